```python
import jax
import jax.numpy as jnp
from jax import lax
import numpy as np

D_MODEL = 1024
BATCH = 8
SEQ = 2048
DEPTH = 4

N_META = 16
D_MIX = 2 * D_MODEL
W_GRP = D_MIX // 4
CONV_K = 4
D_FF = ((8 * D_MODEL // 3 + 127) // 128) * 128
EPS = 1e-6
CHUNK = 64
LEAD_PAD = CHUNK - N_META

LRU_HEAD_DIM = 64
LRU_HEADS = W_GRP // LRU_HEAD_DIM
LRU_C = 8.0

GDN_HEAD_DIM = 128
GDN_HEADS = W_GRP // GDN_HEAD_DIM

SSD_HEAD_DIM = 64
SSD_HEADS = W_GRP // SSD_HEAD_DIM
SSD_GROUPS = 2
SSD_STATE = 128

S5_GROUP_CH = 16
S5_GROUPS = W_GRP // S5_GROUP_CH
S5_STATE = 64

IN_SPLITS = (W_GRP, W_GRP, 3 * W_GRP, W_GRP, GDN_HEADS, GDN_HEADS,
             W_GRP, W_GRP + 2 * SSD_GROUPS * SSD_STATE, SSD_HEADS, W_GRP)
D_IN = sum(IN_SPLITS)

kernel_name = "hymba_style_parallel_hybrid_trunk"

F32 = jnp.float32


def rms_norm(x, g):
    xf = x.astype(F32)
    y = xf * lax.rsqrt(jnp.mean(xf * xf, axis=-1, keepdims=True) + EPS)
    return (y * g.astype(F32)).astype(x.dtype)


def l2_norm(x):
    return x * lax.rsqrt(jnp.sum(x * x, axis=-1, keepdims=True) + EPS)


def causal_dwconv(x, w):
    ch = x.shape[-1]
    return lax.conv_general_dilated(
        x, w[:, None, :].astype(x.dtype), window_strides=(1,),
        padding=[(w.shape[0] - 1, 0)], dimension_numbers=("NWC", "WIO", "NWC"),
        feature_group_count=ch)


def swiglu(x, w_gate, w_up, w_down):
    return (jax.nn.silu(x @ w_gate) * (x @ w_up)) @ w_down


def front_pad(t, n):
    return jnp.pad(t, [(0, 0), (n, 0)] + [(0, 0)] * (t.ndim - 2))


def _linear_combine(e1, e2):
    a1, b1 = e1
    a2, b2 = e2
    return a1 * a2, a2 * b1 + b2


def _complex_linear_combine(e1, e2):
    a1r, a1i, b1r, b1i = e1
    a2r, a2i, b2r, b2i = e2
    return (a1r * a2r - a1i * a2i, a1r * a2i + a1i * a2r,
            a2r * b1r - a2i * b1i + b2r, a2r * b1i + a2i * b1r + b2i)


def rglru_mixer(u_x, u_gate, conv_w, conv_b, w_a, b_a, w_i, b_i, lam, norm_g):
    bsz, t, _ = u_x.shape
    xc = (causal_dwconv(u_x, conv_w) + conv_b).astype(F32)
    xh = xc.reshape(bsz, t, LRU_HEADS, LRU_HEAD_DIM)
    r = jax.nn.sigmoid(jnp.einsum("btsi,sij->btsj", xh, w_a.astype(F32)).reshape(bsz, t, W_GRP) + b_a)
    ig = jax.nn.sigmoid(jnp.einsum("btsi,sij->btsj", xh, w_i.astype(F32)).reshape(bsz, t, W_GRP) + b_i)
    log_a = -LRU_C * r * jax.nn.softplus(-lam.astype(F32))
    a = jnp.exp(log_a)
    b = jnp.sqrt(-jnp.expm1(2.0 * log_a)) * (ig * xc)
    _, h = lax.associative_scan(_linear_combine, (a, b), axis=1)
    y = jax.nn.gelu(u_gate.astype(F32)) * h
    return rms_norm(y, norm_g)


def chunk_gated_delta_rule(q, k, v, beta, g):
    bsz, tp, nh, dk = q.shape
    dv = v.shape[-1]
    nc = tp // CHUNK

    def blk(z):
        z = z.reshape((bsz, nc, CHUNK) + z.shape[2:])
        return jnp.moveaxis(jnp.moveaxis(z, 1, 0), 2, 3)

    q, k, v, beta, g = (blk(z) for z in (q, k, v, beta, g))
    g = jnp.cumsum(g, axis=-1)
    incl = jnp.tril(jnp.ones((CHUNK, CHUNK), dtype=bool))
    strict = jnp.tril(jnp.ones((CHUNK, CHUNK), dtype=bool), -1)
    decay = jnp.exp(jnp.where(incl, g[..., :, None] - g[..., None, :], -jnp.inf))
    k_beta = k * beta[..., None]
    lmat = jnp.where(strict, jnp.einsum("nbhcd,nbhsd->nbhcs", k_beta, k) * decay, 0.0)
    eye = jnp.eye(CHUNK, dtype=F32)
    rhs = jnp.concatenate([v * beta[..., None], k_beta * jnp.exp(g)[..., None]], axis=-1)
    sol = lax.linalg.triangular_solve(eye + lmat, rhs, left_side=True, lower=True,
                                      unit_diagonal=True)
    u, w = sol[..., :dv], sol[..., dv:]
    attn = jnp.einsum("nbhcd,nbhsd->nbhcs", q, k) * decay
    q_dec = q * jnp.exp(g)[..., None]
    k_dec = k * jnp.exp(g[..., -1:] - g)[..., None]
    last = jnp.exp(g[..., -1])

    def step(s, inp):
        u_n, w_n, attn_n, q_n, k_n, last_n = inp
        v_new = u_n - jnp.einsum("bhcd,bhde->bhce", w_n, s)
        o_n = jnp.einsum("bhcd,bhde->bhce", q_n, s) + jnp.einsum("bhcs,bhse->bhce", attn_n, v_new)
        s = s * last_n[..., None, None] + jnp.einsum("bhcd,bhce->bhde", k_n, v_new)
        return s, o_n

    s0 = jnp.zeros((bsz, nh, dk, dv), F32)
    _, o = lax.scan(step, s0, (u, w, attn, q_dec, k_dec, last))
    return jnp.moveaxis(o, 0, 1).transpose(0, 1, 3, 2, 4).reshape(bsz, tp, nh, dv)


def gdn_mixer(u_qkv, u_z, u_beta, u_alpha, conv_w, a_log, dt_bias, norm_g):
    bsz, t, _ = u_qkv.shape
    qkv = jax.nn.silu(causal_dwconv(u_qkv, conv_w).astype(F32))
    q, k, v = jnp.split(qkv, 3, axis=-1)
    hs = (bsz, t, GDN_HEADS, GDN_HEAD_DIM)
    q = l2_norm(q.reshape(hs)) * (GDN_HEAD_DIM ** -0.5)
    k = l2_norm(k.reshape(hs))
    v = v.reshape(hs)
    beta = jax.nn.sigmoid(u_beta.astype(F32))
    g = -jnp.exp(a_log.astype(F32)) * jax.nn.softplus(u_alpha.astype(F32) + dt_bias)
    o = chunk_gated_delta_rule(*(front_pad(z, LEAD_PAD) for z in (q, k, v, beta, g)))[:, LEAD_PAD:]
    o = rms_norm(o, norm_g) * jax.nn.silu(u_z.astype(F32).reshape(hs))
    return o.reshape(bsz, t, W_GRP)


def ssd_chunked(x, a_dt, bm, cm):
    bsz, tp, nh, hd = x.shape
    nc = tp // CHUNK
    hpg = nh // SSD_GROUPS
    x = x.reshape(bsz, nc, CHUNK, SSD_GROUPS, hpg, hd)
    a = a_dt.reshape(bsz, nc, CHUNK, SSD_GROUPS, hpg).transpose(0, 1, 3, 4, 2)
    bm = bm.reshape(bsz, nc, CHUNK, SSD_GROUPS, SSD_STATE)
    cm = cm.reshape(bsz, nc, CHUNK, SSD_GROUPS, SSD_STATE)
    a_cum = jnp.cumsum(a, axis=-1)
    incl = jnp.tril(jnp.ones((CHUNK, CHUNK), dtype=bool))
    lmat = jnp.exp(jnp.where(incl, a_cum[..., :, None] - a_cum[..., None, :], -jnp.inf))
    cb = jnp.einsum("bclgn,bcsgn->bcgls", cm, bm)
    y_diag = jnp.einsum("bcgls,bcgels,bcsgep->bclgep", cb, lmat, x)
    decay_states = jnp.exp(a_cum[..., -1:] - a_cum)
    states = jnp.einsum("bclgn,bcgel,bclgep->bcgepn", bm, decay_states, x)
    chunk_decay = jnp.exp(a_cum[..., -1])

    def step(s, inp):
        st, dec = inp
        return s * dec[..., None, None] + st, s

    _, s_in = lax.scan(step, jnp.zeros_like(states[:, 0]),
                       (jnp.moveaxis(states, 1, 0), jnp.moveaxis(chunk_decay, 1, 0)))
    s_in = jnp.moveaxis(s_in, 0, 1)
    y_off = jnp.einsum("bclgn,bcgepn,bcgel->bclgep", cm, s_in, jnp.exp(a_cum))
    return (y_diag + y_off).reshape(bsz, tp, nh, hd)


def ssd_mixer(u_z, u_xbc, u_dt, conv_w, conv_b, a_log, dt_bias, d_skip, norm_g):
    bsz, t, _ = u_z.shape
    xbc = jax.nn.silu((causal_dwconv(u_xbc, conv_w) + conv_b).astype(F32))
    xs, bm, cm = jnp.split(xbc, [W_GRP, W_GRP + SSD_GROUPS * SSD_STATE], axis=-1)
    xs = xs.reshape(bsz, t, SSD_HEADS, SSD_HEAD_DIM)
    bm = bm.reshape(bsz, t, SSD_GROUPS, SSD_STATE)
    cm = cm.reshape(bsz, t, SSD_GROUPS, SSD_STATE)
    dt = jax.nn.softplus(u_dt.astype(F32) + dt_bias)
    a = -jnp.exp(a_log.astype(F32))
    y = ssd_chunked(*(front_pad(z, LEAD_PAD) for z in (xs * dt[..., None], dt * a, bm, cm)))[:, LEAD_PAD:]
    y = y + d_skip[:, None] * xs
    gs = (bsz, t, SSD_GROUPS, W_GRP // SSD_GROUPS)
    y = y.reshape(gs) * jax.nn.silu(u_z.astype(F32).reshape(gs))
    y = rms_norm(y, norm_g.reshape(SSD_GROUPS, W_GRP // SSD_GROUPS))
    return y.reshape(bsz, t, W_GRP)


def s5_mixer(u, a_re, a_im, log_dt, b_re, b_im, c_re, c_im, d_skip, w_glu, norm_g):
    bsz, t, _ = u.shape
    uf = u.astype(F32)
    ug = uf.reshape(bsz, t, S5_GROUPS, S5_GROUP_CH)
    lam_re = jnp.minimum(a_re.astype(F32), -1e-4)
    lam_im = a_im.astype(F32)
    dt = jnp.exp(log_dt.astype(F32))[:, None]
    mag = jnp.exp(dt * lam_re)
    ab_re = mag * jnp.cos(dt * lam_im)
    ab_im = mag * jnp.sin(dt * lam_im)
    den = lam_re * lam_re + lam_im * lam_im
    f_re = ((ab_re - 1.0) * lam_re + ab_im * lam_im) / den
    f_im = (ab_im * lam_re - (ab_re - 1.0) * lam_im) / den
    bb_re = f_re[..., None] * b_re - f_im[..., None] * b_im
    bb_im = f_re[..., None] * b_im + f_im[..., None] * b_re
    bu_re = jnp.einsum("btgi,gpi->tbgp", ug, bb_re)
    bu_im = jnp.einsum("btgi,gpi->tbgp", ug, bb_im)
    shp = (t, 1, S5_GROUPS, S5_STATE)
    _, _, s_re, s_im = lax.associative_scan(
        _complex_linear_combine,
        (jnp.broadcast_to(ab_re, shp), jnp.broadcast_to(ab_im, shp), bu_re, bu_im), axis=0)
    y = jnp.einsum("tbgp,gip->btgi", s_re, c_re) - jnp.einsum("tbgp,gip->btgi", s_im, c_im)
    y = y.reshape(bsz, t, W_GRP) + d_skip * uf
    y = jax.nn.gelu(y)
    y = y * jax.nn.sigmoid(y @ w_glu.astype(F32))
    return rms_norm(y, norm_g)


def hybrid_mixer(h, w_in, w_out, lru_p, gdn_p, ssd_p, s5_p):
    proj = h @ w_in
    (a_x, a_gate, b_qkv, b_z, b_beta, b_alpha, c_z, c_xbc, c_dt, d_u) = jnp.split(
        proj, np.cumsum(IN_SPLITS)[:-1].tolist(), axis=-1)
    y_a = rglru_mixer(a_x, a_gate, *lru_p)
    y_b = gdn_mixer(b_qkv, b_z, b_beta, b_alpha, *gdn_p)
    y_c = ssd_mixer(c_z, c_xbc, c_dt, *ssd_p)
    y_d = s5_mixer(d_u, *s5_p)
    y = jnp.concatenate([y_a, y_b, y_c, y_d], axis=-1).astype(h.dtype)
    return y @ w_out


def setup_inputs(seed: int = 0) -> dict:
    key = jax.random.key(seed)
    ks = iter(jax.random.split(key, 64))
    L = DEPTH

    def nrm(shape, scale):
        return jax.random.normal(next(ks), shape, F32) * scale

    def gain(shape):
        return 1.0 + 0.02 * jax.random.normal(next(ks), shape, F32)

    def unif(shape, lo, hi):
        return jax.random.uniform(next(ks), shape, F32, lo, hi)

    def dt_bias(shape):
        dt = jnp.exp(unif(shape, float(np.log(1e-3)), float(np.log(1e-1))))
        return dt + jnp.log(-jnp.expm1(-dt))

    a0 = unif((L, W_GRP), 0.9, 0.999) ** (1.0 / LRU_C)
    xbc_w = W_GRP + 2 * SSD_GROUPS * SSD_STATE
    return {
        "x": nrm((BATCH, SEQ, D_MODEL), 1.0),
        "meta_tokens": nrm((N_META, D_MODEL), 1.0),
        "ffn1_norm": gain((L, D_MODEL)),
        "ffn1_w_gate": nrm((L, D_MODEL, D_FF), D_MODEL ** -0.5),
        "ffn1_w_up": nrm((L, D_MODEL, D_FF), D_MODEL ** -0.5),
        "ffn1_w_down": nrm((L, D_FF, D_MODEL), D_FF ** -0.5),
        "mix_norm": gain((L, D_MODEL)),
        "w_in": nrm((L, D_MODEL, D_IN), D_MODEL ** -0.5),
        "w_out": nrm((L, D_MIX, D_MODEL), D_MIX ** -0.5),
        "lru_conv_w": nrm((L, CONV_K, W_GRP), CONV_K ** -0.5),
        "lru_conv_b": nrm((L, W_GRP), 0.02),
        "lru_w_a": nrm((L, LRU_HEADS, LRU_HEAD_DIM, LRU_HEAD_DIM), LRU_HEAD_DIM ** -0.5),
        "lru_b_a": nrm((L, W_GRP), 0.02),
        "lru_w_i": nrm((L, LRU_HEADS, LRU_HEAD_DIM, LRU_HEAD_DIM), LRU_HEAD_DIM ** -0.5),
        "lru_b_i": nrm((L, W_GRP), 0.02),
        "lru_lambda": jnp.log(a0) - jnp.log1p(-a0),
        "lru_norm": gain((L, W_GRP)),
        "gdn_conv_w": nrm((L, CONV_K, 3 * W_GRP), CONV_K ** -0.5),
        "gdn_a_log": jnp.log(unif((L, GDN_HEADS), 1.0, 16.0)),
        "gdn_dt_bias": dt_bias((L, GDN_HEADS)),
        "gdn_norm": gain((L, GDN_HEAD_DIM)),
        "ssd_conv_w": nrm((L, CONV_K, xbc_w), CONV_K ** -0.5),
        "ssd_conv_b": nrm((L, xbc_w), 0.02),
        "ssd_a_log": jnp.log(unif((L, SSD_HEADS), 1.0, 16.0)),
        "ssd_dt_bias": dt_bias((L, SSD_HEADS)),
        "ssd_d": gain((L, SSD_HEADS)),
        "ssd_norm": gain((L, W_GRP)),
        "s5_a_re": -0.5 + nrm((L, S5_GROUPS, S5_STATE), 0.01),
        "s5_a_im": jnp.pi * jnp.arange(S5_STATE, dtype=F32) + nrm((L, S5_GROUPS, S5_STATE), 0.01),
        "s5_log_dt": unif((L, S5_GROUPS), float(np.log(1e-3)), float(np.log(1e-1))),
        "s5_b_re": nrm((L, S5_GROUPS, S5_STATE, S5_GROUP_CH), (2 * S5_GROUP_CH) ** -0.5),
        "s5_b_im": nrm((L, S5_GROUPS, S5_STATE, S5_GROUP_CH), (2 * S5_GROUP_CH) ** -0.5),
        "s5_c_re": nrm((L, S5_GROUPS, S5_GROUP_CH, S5_STATE), (2 * S5_STATE) ** -0.5),
        "s5_c_im": nrm((L, S5_GROUPS, S5_GROUP_CH, S5_STATE), (2 * S5_STATE) ** -0.5),
        "s5_d": nrm((L, W_GRP), 1.0),
        "s5_w_glu": nrm((L, W_GRP, W_GRP), W_GRP ** -0.5),
        "s5_norm": gain((L, W_GRP)),
        "ffn2_norm": gain((L, D_MODEL)),
        "ffn2_w_gate": nrm((L, D_MODEL, D_FF), D_MODEL ** -0.5),
        "ffn2_w_up": nrm((L, D_MODEL, D_FF), D_MODEL ** -0.5),
        "ffn2_w_down": nrm((L, D_FF, D_MODEL), D_FF ** -0.5),
        "final_norm": gain((D_MODEL,)),
    }


def reference(x, meta_tokens, ffn1_norm, ffn1_w_gate, ffn1_w_up, ffn1_w_down, mix_norm, w_in, w_out,
              lru_conv_w, lru_conv_b, lru_w_a, lru_b_a, lru_w_i, lru_b_i, lru_lambda, lru_norm,
              gdn_conv_w, gdn_a_log, gdn_dt_bias, gdn_norm,
              ssd_conv_w, ssd_conv_b, ssd_a_log, ssd_dt_bias, ssd_d, ssd_norm,
              s5_a_re, s5_a_im, s5_log_dt, s5_b_re, s5_b_im, s5_c_re, s5_c_im, s5_d, s5_w_glu, s5_norm,
              ffn2_norm, ffn2_w_gate, ffn2_w_up, ffn2_w_down, final_norm):
    bsz = x.shape[0]
    meta = jnp.broadcast_to(meta_tokens.astype(x.dtype)[None], (bsz, N_META, D_MODEL))
    h = jnp.concatenate([meta, x], axis=1)
    for l in range(DEPTH):
        h = h + 0.5 * swiglu(rms_norm(h, ffn1_norm[l]), ffn1_w_gate[l], ffn1_w_up[l], ffn1_w_down[l])
        lru_p = (lru_conv_w[l], lru_conv_b[l], lru_w_a[l], lru_b_a[l], lru_w_i[l], lru_b_i[l],
                 lru_lambda[l], lru_norm[l])
        gdn_p = (gdn_conv_w[l], gdn_a_log[l], gdn_dt_bias[l], gdn_norm[l])
        ssd_p = (ssd_conv_w[l], ssd_conv_b[l], ssd_a_log[l], ssd_dt_bias[l], ssd_d[l], ssd_norm[l])
        s5_p = (s5_a_re[l], s5_a_im[l], s5_log_dt[l], s5_b_re[l], s5_b_im[l], s5_c_re[l], s5_c_im[l],
                s5_d[l], s5_w_glu[l], s5_norm[l])
        h = h + hybrid_mixer(rms_norm(h, mix_norm[l]), w_in[l], w_out[l], lru_p, gdn_p, ssd_p, s5_p)
        h = h + 0.5 * swiglu(rms_norm(h, ffn2_norm[l]), ffn2_w_gate[l], ffn2_w_up[l], ffn2_w_down[l])
    return rms_norm(h, final_norm)[:, N_META:]
```

```python
import functools

import numpy as np
import jax
import jax.numpy as jnp
from jax import lax
from jax.experimental import pallas as pl
from jax.experimental.pallas import tpu as pltpu

F32 = jnp.float32
BF16 = jnp.bfloat16
HI = lax.Precision.HIGHEST

D_MODEL = 1024
N_META = 16
W_GRP = 512
D_MIX = 4 * W_GRP
CONV_K = 4
D_FF = 2816
EPS = 1e-6
CHUNK = 64
LEAD_PAD = CHUNK - N_META
LRU_C = 8.0
GDN_HEADS = 4
GDN_HEAD_DIM = 128
SSD_HEADS = 8
SSD_HEAD_DIM = 64
SSD_GROUPS = 2
SSD_STATE = 128
S5_GROUP_CH = 16
S5_GROUPS = 32
S5_STATE = 64
S5_SUB = 16
S5_PAIRS = S5_GROUPS // 2

LANES = 128
SMALL_W = LANES
BETA_OFF, ALPHA_OFF, DT_OFF = 0, 4, 8
VMEM_LIMIT = 56 * 1024 * 1024

NN = (((1,), (0,)), ((), ()))
NT = (((1,), (1,)), ((), ()))


def _dot(a, b, dims=NN, prec=None):
    return lax.dot_general(a, b, dims, precision=prec, preferred_element_type=F32)


def _rms(x, g):
    return x * lax.rsqrt(jnp.mean(x * x, axis=-1, keepdims=True) + EPS) * g


def _silu(x):
    return x * jax.nn.sigmoid(x)


def _gelu_tanh(x):
    return 0.5 * x * (1.0 + jnp.tanh(np.sqrt(2.0 / np.pi).astype(np.float32) * (x + 0.044715 * (x * x * x))))


def _softplus(x):
    return jnp.maximum(x, 0.0) + jnp.log1p(jnp.exp(-jnp.abs(x)))


def _row_iota(shape):
    return lax.broadcasted_iota(jnp.int32, shape, 0)


def _col_iota(shape):
    return lax.broadcasted_iota(jnp.int32, shape, 1)


def _shift_rows(x, prev8, k):
    xs = pltpu.roll(x, k, axis=0)
    ps = pltpu.roll(prev8, k, axis=0)
    top = jnp.where(_row_iota(ps.shape) < k, ps, xs[:8])
    return jnp.concatenate([top, xs[8:]], axis=0)


def _causal_conv4(x, prev8, w_ref):
    y = x * w_ref[CONV_K - 1:CONV_K, :]
    for k in range(1, CONV_K):
        y = y + _shift_rows(x, prev8, k) * w_ref[CONV_K - 1 - k:CONV_K - k, :]
    return y


def _compiler_params(sem):
    return pltpu.CompilerParams(dimension_semantics=sem, vmem_limit_bytes=VMEM_LIMIT)


def _const_spec(shape):
    nd = len(shape)
    return pl.BlockSpec(shape, lambda *_: (0,) * nd)


FFN_TM = 512
FFN_FC = 256


def _ffn_body(h_ref, g_ref, wg_ref, wu_ref, wd_ref, o_ref):
    h = h_ref[...]
    xn = _rms(h, g_ref[...]).astype(BF16)
    acc = jnp.zeros_like(h)
    for c in range(D_FF // FFN_FC):
        sl = slice(c * FFN_FC, (c + 1) * FFN_FC)
        gt = _dot(xn, wg_ref[:, sl])
        up = _dot(xn, wu_ref[:, sl])
        acc = acc + _dot((_silu(gt) * up).astype(BF16), wd_ref[sl, :])
    o_ref[...] = h + 0.5 * acc


def _ffn(h, g, wg, wu, wd):
    m = h.shape[0]
    tok = pl.BlockSpec((FFN_TM, D_MODEL), lambda i: (i, 0))
    return pl.pallas_call(
        _ffn_body,
        grid=(m // FFN_TM,),
        in_specs=[tok, _const_spec((1, D_MODEL)), _const_spec((D_MODEL, D_FF)),
                  _const_spec((D_MODEL, D_FF)), _const_spec((D_FF, D_MODEL))],
        out_specs=tok,
        out_shape=jax.ShapeDtypeStruct(h.shape, F32),
        compiler_params=_compiler_params(("parallel",)),
        name="ffn",
    )(h, g, wg, wu, wd)


PROJ_TM = 256
PROJ_SLABS = (W_GRP, W_GRP, 3 * W_GRP, W_GRP, W_GRP, W_GRP + 2 * SSD_GROUPS * SSD_STATE, W_GRP, SMALL_W)
PROJ_W = sum(PROJ_SLABS)


def _inproj_body(h_ref, g_ref, w_ref, *out_refs):
    xn = _rms(h_ref[...], g_ref[...]).astype(BF16)
    off = 0
    for o_ref in out_refs:
        n = o_ref.shape[-1]
        o_ref[...] = _dot(xn, w_ref[:, off:off + n])
        off += n


def _inproj(h, g, w):
    m = h.shape[0]
    return pl.pallas_call(
        _inproj_body,
        grid=(m // PROJ_TM,),
        in_specs=[pl.BlockSpec((PROJ_TM, D_MODEL), lambda i: (i, 0)), _const_spec((1, D_MODEL)),
                  _const_spec((D_MODEL, PROJ_W))],
        out_specs=[pl.BlockSpec((PROJ_TM, n), lambda i: (i, 0)) for n in PROJ_SLABS],
        out_shape=[jax.ShapeDtypeStruct((m, n), F32) for n in PROJ_SLABS],
        compiler_params=_compiler_params(("parallel",)),
        name="inproj",
    )(h, g, w)


MIX_TT = 704


def _valid_rows(t_idx, r0, ncols):
    rows = _row_iota((CHUNK, ncols)) + (t_idx * MIX_TT + r0)
    return rows >= LEAD_PAD


def _seq_spec(ncols):
    return pl.BlockSpec((None, MIX_TT, ncols), lambda b, t: (b, t, 0))


def _lru_body(ux_ref, ug_ref, cw_ref, cb_ref, wai_ref, bai_ref, lam_ref, ng_ref, o_ref,
              prev_ref, hst_ref):
    t_idx = pl.program_id(1)

    @pl.when(t_idx == 0)
    def _():
        prev_ref[...] = jnp.zeros_like(prev_ref)
        hst_ref[...] = jnp.zeros_like(hst_ref)

    neg_c_sp = -LRU_C * _softplus(-lam_ref[...])

    def chunk(c, carry):
        r0 = pl.multiple_of(c * CHUNK, CHUNK)
        x = ux_ref[pl.ds(r0, CHUNK), :]
        xc = _causal_conv4(x, prev_ref[...], cw_ref) + cb_ref[...]
        prev_ref[...] = x[CHUNK - 8:]
        gates = _dot(xc.astype(BF16), wai_ref[...]) + bai_ref[...]
        r = jax.nn.sigmoid(gates[:, :W_GRP])
        ig = jax.nn.sigmoid(gates[:, W_GRP:])
        log_a = neg_c_sp * r
        a = jnp.exp(log_a)
        bv = jnp.sqrt(jnp.tanh(-log_a) * (1.0 + a * a)) * (ig * xc)
        bv = jnp.where(_valid_rows(t_idx, r0, W_GRP), bv, 0.0)
        rows = _row_iota((CHUNK, W_GRP))
        s = 1
        while s < CHUNK:
            a_sh = jnp.where(rows < s, 1.0, pltpu.roll(a, s, axis=0))
            b_sh = jnp.where(rows < s, 0.0, pltpu.roll(bv, s, axis=0))
            bv = a * b_sh + bv
            a = a * a_sh
            s *= 2
        hcur = bv + a * hst_ref[0:1, :]
        hst_ref[...] = jnp.broadcast_to(hcur[CHUNK - 1:CHUNK, :], hst_ref.shape)
        y = _gelu_tanh(ug_ref[pl.ds(r0, CHUNK), :]) * hcur
        y = _rms(y, ng_ref[...])
        y = jnp.where(_valid_rows(t_idx, r0, W_GRP), y, 0.0)
        o_ref[pl.ds(r0, CHUNK), :] = y.astype(o_ref.dtype)
        return carry

    lax.fori_loop(0, MIX_TT // CHUNK, chunk, 0)


def _lru(ux, ug, cw, cb, wai, bai, lam, ng):
    b, tp, _ = ux.shape
    return pl.pallas_call(
        _lru_body,
        grid=(b, tp // MIX_TT),
        in_specs=[_seq_spec(W_GRP), _seq_spec(W_GRP), _const_spec(cw.shape), _const_spec(cb.shape),
                  _const_spec(wai.shape), _const_spec(bai.shape), _const_spec(lam.shape),
                  _const_spec(ng.shape)],
        out_specs=_seq_spec(W_GRP),
        out_shape=jax.ShapeDtypeStruct((b, tp, W_GRP), BF16),
        scratch_shapes=[pltpu.VMEM((8, W_GRP), F32), pltpu.VMEM((8, W_GRP), F32)],
        compiler_params=_compiler_params(("parallel", "arbitrary")),
        name="lru",
    )(ux, ug, cw, cb, wai, bai, lam, ng)


def _tri_masks():
    r = _row_iota((CHUNK, CHUNK))
    c = _col_iota((CHUNK, CHUNK))
    return r >= c, r > c


def _unit_lower_inverse(n):
    eye = (_row_iota((CHUNK, CHUNK)) == _col_iota((CHUNK, CHUNK))).astype(F32)
    p = eye - n
    m = _dot(n, n, prec=HI)
    s = 2
    while True:
        p = p + _dot(p, m, prec=HI)
        s *= 2
        if s >= CHUNK:
            break
        m = _dot(m, m, prec=HI)
    return p


def _gdn_body(qkv_ref, z_ref, sm_ref, cw_ref, alog_ref, dtb_ref, ng_ref, o_ref, prev_ref, s_ref):
    t_idx = pl.program_id(1)

    @pl.when(t_idx == 0)
    def _():
        prev_ref[...] = jnp.zeros_like(prev_ref)
        s_ref[...] = jnp.zeros_like(s_ref)

    incl, strict = _tri_masks()
    tril = incl.astype(F32)
    triu = (_row_iota((CHUNK, CHUNK)) <= _col_iota((CHUNK, CHUNK))).astype(F32)
    neg_a = -jnp.exp(alog_ref[...])
    hd = GDN_HEAD_DIM

    def chunk(c, carry):
        r0 = pl.multiple_of(c * CHUNK, CHUNK)
        x = qkv_ref[pl.ds(r0, CHUNK), :]
        qkv = _silu(_causal_conv4(x, prev_ref[...], cw_ref))
        prev_ref[...] = x[CHUNK - 8:]
        sm = sm_ref[pl.ds(r0, CHUNK), :]
        beta_all = jax.nn.sigmoid(sm)
        g_all = neg_a * _softplus(sm + dtb_ref[...])
        g_all = jnp.where(_valid_rows(t_idx, r0, SMALL_W), g_all, 0.0)
        gc_all = _dot(tril, g_all, prec=HI)
        gc_all_t = _dot(g_all.T, triu, prec=HI)
        valid = _valid_rows(t_idx, r0, hd)
        for h in range(GDN_HEADS):
            q = qkv[:, h * hd:(h + 1) * hd]
            k = qkv[:, W_GRP + h * hd:W_GRP + (h + 1) * hd]
            v = qkv[:, 2 * W_GRP + h * hd:2 * W_GRP + (h + 1) * hd]
            q = q * lax.rsqrt(jnp.sum(q * q, axis=-1, keepdims=True) + EPS) * (hd ** -0.5)
            k = k * lax.rsqrt(jnp.sum(k * k, axis=-1, keepdims=True) + EPS)
            beta = beta_all[:, BETA_OFF + h:BETA_OFF + h + 1]
            gcol = gc_all[:, ALPHA_OFF + h:ALPHA_OFF + h + 1]
            grow = gc_all_t[ALPHA_OFF + h:ALPHA_OFF + h + 1, :]
            decay = jnp.exp(jnp.where(incl, gcol - grow, -jnp.inf))
            eg = jnp.exp(gcol)
            g_last = gcol[CHUNK - 1:CHUNK, :]
            kb = k * beta
            lmat = jnp.where(strict, _dot(kb, k, NT, prec=HI) * decay, 0.0)
            tinv = _unit_lower_inverse(lmat)
            u = _dot(tinv, v * beta, prec=HI)
            w = _dot(tinv, kb * eg, prec=HI)
            attn = _dot(q, k, NT, prec=HI) * decay
            s = s_ref[h]
            v_new = u - _dot(w, s, prec=HI)
            o = _dot(q * eg, s, prec=HI) + _dot(attn, v_new, prec=HI)
            k_dec = k * jnp.exp(g_last - gcol)
            s_ref[h] = s * jnp.exp(g_last) + _dot(k_dec.T, v_new, prec=HI)
            o = _rms(o, ng_ref[...]) * _silu(z_ref[pl.ds(r0, CHUNK), h * hd:(h + 1) * hd])
            o = jnp.where(valid, o, 0.0)
            o_ref[pl.ds(r0, CHUNK), h * hd:(h + 1) * hd] = o.astype(o_ref.dtype)
        return carry

    lax.fori_loop(0, MIX_TT // CHUNK, chunk, 0)


def _gdn(qkv, z, sm, cw, alog, dtb, ng):
    b, tp, _ = qkv.shape
    return pl.pallas_call(
        _gdn_body,
        grid=(b, tp // MIX_TT),
        in_specs=[_seq_spec(3 * W_GRP), _seq_spec(W_GRP), _seq_spec(SMALL_W), _const_spec(cw.shape),
                  _const_spec(alog.shape), _const_spec(dtb.shape), _const_spec(ng.shape)],
        out_specs=_seq_spec(W_GRP),
        out_shape=jax.ShapeDtypeStruct((b, tp, W_GRP), BF16),
        scratch_shapes=[pltpu.VMEM((8, 3 * W_GRP), F32),
                        pltpu.VMEM((GDN_HEADS, GDN_HEAD_DIM, GDN_HEAD_DIM), F32)],
        compiler_params=_compiler_params(("parallel", "arbitrary")),
        name="gdn",
    )(qkv, z, sm, cw, alog, dtb, ng)


XBC_W = W_GRP + 2 * SSD_GROUPS * SSD_STATE


def _ssd_body(z_ref, xbc_ref, sm_ref, cw_ref, cb_ref, alog_ref, dtb_ref, dsk_ref, ng_ref, o_ref,
              prev_ref, s_ref):
    t_idx = pl.program_id(1)

    @pl.when(t_idx == 0)
    def _():
        prev_ref[...] = jnp.zeros_like(prev_ref)
        s_ref[...] = jnp.zeros_like(s_ref)

    incl, _ = _tri_masks()
    tril = incl.astype(F32)
    triu = (_row_iota((CHUNK, CHUNK)) <= _col_iota((CHUNK, CHUNK))).astype(F32)
    neg_a = -jnp.exp(alog_ref[...])
    hp = SSD_HEAD_DIM
    expand = (_row_iota((SMALL_W, W_GRP)) - DT_OFF
              == _col_iota((SMALL_W, W_GRP)) >> (hp.bit_length() - 1)).astype(F32)
    gw = W_GRP // SSD_GROUPS
    hpg = SSD_HEADS // SSD_GROUPS

    def chunk(c, carry):
        r0 = pl.multiple_of(c * CHUNK, CHUNK)
        x = xbc_ref[pl.ds(r0, CHUNK), :]
        xbc = _silu(_causal_conv4(x, prev_ref[...], cw_ref) + cb_ref[...])
        prev_ref[...] = x[CHUNK - 8:]
        xs = xbc[:, :W_GRP]
        dt_all = _softplus(sm_ref[pl.ds(r0, CHUNK), :] + dtb_ref[...])
        dt_all = jnp.where(_valid_rows(t_idx, r0, SMALL_W), dt_all, 0.0)
        adt_all = dt_all * neg_a
        x_in = xs * _dot(dt_all, expand, prec=HI)
        acum = _dot(tril, adt_all, prec=HI)
        acum_t = _dot(adt_all.T, triu, prec=HI)
        ys = []
        for g in range(SSD_GROUPS):
            bg = xbc[:, W_GRP + g * SSD_STATE:W_GRP + (g + 1) * SSD_STATE]
            cg = xbc[:, W_GRP + (SSD_GROUPS + g) * SSD_STATE:W_GRP + (SSD_GROUPS + g + 1) * SSD_STATE]
            cb = _dot(cg, bg, NT, prec=HI)
            for e in range(hpg):
                h = g * hpg + e
                acol = acum[:, DT_OFF + h:DT_OFF + h + 1]
                arow = acum_t[DT_OFF + h:DT_OFF + h + 1, :]
                a_last = acol[CHUNK - 1:CHUNK, :]
                lmat = jnp.exp(jnp.where(incl, acol - arow, -jnp.inf))
                xh = x_in[:, h * hp:(h + 1) * hp]
                s = s_ref[h]
                y = _dot(cb * lmat, xh, prec=HI) + _dot(cg, s, NT, prec=HI) * jnp.exp(acol)
                st = _dot((xh * jnp.exp(a_last - acol)).T, bg, prec=HI)
                s_ref[h] = s * jnp.exp(a_last) + st
                ys.append(y)
        y = jnp.concatenate(ys, axis=1) + dsk_ref[...] * xs
        y = y * _silu(z_ref[pl.ds(r0, CHUNK), :])
        y = jnp.concatenate(
            [_rms(y[:, g * gw:(g + 1) * gw], ng_ref[:, g * gw:(g + 1) * gw]) for g in range(SSD_GROUPS)],
            axis=1)
        y = jnp.where(_valid_rows(t_idx, r0, W_GRP), y, 0.0)
        o_ref[pl.ds(r0, CHUNK), :] = y.astype(o_ref.dtype)
        return carry

    lax.fori_loop(0, MIX_TT // CHUNK, chunk, 0)


def _ssd(z, xbc, sm, cw, cb, alog, dtb, dsk, ng):
    b, tp, _ = z.shape
    return pl.pallas_call(
        _ssd_body,
        grid=(b, tp // MIX_TT),
        in_specs=[_seq_spec(W_GRP), _seq_spec(XBC_W), _seq_spec(SMALL_W), _const_spec(cw.shape),
                  _const_spec(cb.shape), _const_spec(alog.shape), _const_spec(dtb.shape),
                  _const_spec(dsk.shape), _const_spec(ng.shape)],
        out_specs=_seq_spec(W_GRP),
        out_shape=jax.ShapeDtypeStruct((b, tp, W_GRP), BF16),
        scratch_shapes=[pltpu.VMEM((8, XBC_W), F32),
                        pltpu.VMEM((SSD_HEADS, SSD_HEAD_DIM, SSD_STATE), F32)],
        compiler_params=_compiler_params(("parallel", "arbitrary")),
        name="ssd",
    )(z, xbc, sm, cw, cb, alog, dtb, dsk, ng)


S5_IN = S5_SUB * S5_GROUP_CH


def _s5_discretise(a_re, a_im, log_dt):
    lam_re = jnp.minimum(a_re, -1e-4)
    lam_im = a_im
    dt = jnp.exp(log_dt)
    mag = jnp.exp(dt * lam_re)
    ab_re = mag * jnp.cos(dt * lam_im)
    ab_im = mag * jnp.sin(dt * lam_im)
    den = lam_re * lam_re + lam_im * lam_im
    f_re = ((ab_re - 1.0) * lam_re + ab_im * lam_im) / den
    f_im = (ab_im * lam_re - (ab_re - 1.0) * lam_im) / den
    return ab_re, ab_im, f_re, f_im


def _s5_prep_body(ar_ref, ai_ref, ld_ref, arc_ref, aic_ref, ldc_ref, btr_ref, bti_ref, ctr_ref, cti_ref,
                  toep_ref, wor_ref, woi_ref, wcr_ref, wci_ref, a16r_ref, a16i_ref):
    par = pl.program_id(0) % 2
    ab_re, ab_im, f_re, f_im = _s5_discretise(ar_ref[...], ai_ref[...], ld_ref[...])
    abc_re, abc_im, _, _ = _s5_discretise(arc_ref[...], aic_ref[...], ldc_ref[...])
    bb_re = f_re * btr_ref[...] - f_im * bti_ref[...]
    bb_im = f_re * bti_ref[...] + f_im * btr_ref[...]
    ct_re = ctr_ref[...]
    ct_im = cti_ref[...]

    ch_bits = S5_GROUP_CH.bit_length() - 1
    rblk = _row_iota((S5_IN, S5_IN)) >> ch_bits
    cblk = _col_iota((S5_IN, S5_IN)) >> ch_bits
    rblk_p = _row_iota((S5_IN, S5_STATE)) >> ch_bits
    cblk_p = _col_iota((S5_STATE, S5_IN)) >> ch_bits

    p_re = jnp.ones((1, S5_STATE), F32)
    p_im = jnp.zeros((1, S5_STATE), F32)
    q_re = abc_re * jnp.ones((S5_STATE, S5_IN), F32)
    q_im = abc_im * jnp.ones((S5_STATE, S5_IN), F32)
    toep = jnp.zeros((S5_IN, S5_IN), F32)
    pt_re = jnp.zeros((S5_IN, S5_STATE), F32)
    pt_im = jnp.zeros((S5_IN, S5_STATE), F32)
    qt_re = jnp.zeros((S5_STATE, S5_IN), F32)
    qt_im = jnp.zeros((S5_STATE, S5_IN), F32)
    for tau in range(S5_SUB):
        x_re = bb_re * p_re - bb_im * p_im
        x_im = bb_re * p_im + bb_im * p_re
        m = _dot(x_re, ct_re, prec=HI) - _dot(x_im, ct_im, prec=HI)
        toep = jnp.where(cblk - rblk == tau, m, toep)
        pt_re = jnp.where(rblk_p == S5_SUB - 1 - tau, p_re, pt_re)
        pt_im = jnp.where(rblk_p == S5_SUB - 1 - tau, p_im, pt_im)
        qt_re = jnp.where(cblk_p == tau, q_re, qt_re)
        qt_im = jnp.where(cblk_p == tau, q_im, qt_im)
        p_re, p_im = p_re * ab_re - p_im * ab_im, p_re * ab_im + p_im * ab_re
        q_re, q_im = q_re * abc_re - q_im * abc_im, q_re * abc_im + q_im * abc_re
    toep_ref[...] = toep.astype(toep_ref.dtype)
    a16r_ref[...] = p_re
    a16i_ref[...] = p_im

    wo_re = bb_re * pt_re - bb_im * pt_im
    wo_im = bb_re * pt_im + bb_im * pt_re
    zo = jnp.zeros_like(wo_re)
    wor_ref[...] = jnp.where(par == 0, jnp.concatenate([wo_re, zo], axis=1),
                             jnp.concatenate([zo, wo_re], axis=1)).astype(wor_ref.dtype)
    woi_ref[...] = jnp.where(par == 0, jnp.concatenate([wo_im, zo], axis=1),
                             jnp.concatenate([zo, wo_im], axis=1)).astype(woi_ref.dtype)
    wc_re = ct_re * qt_re - ct_im * qt_im
    wc_im = -(ct_re * qt_im + ct_im * qt_re)
    zc = jnp.zeros_like(wc_re)
    wcr_ref[...] = jnp.where(par == 0, jnp.concatenate([wc_re, zc], axis=0),
                             jnp.concatenate([zc, wc_re], axis=0)).astype(wcr_ref.dtype)
    wci_ref[...] = jnp.where(par == 0, jnp.concatenate([wc_im, zc], axis=0),
                             jnp.concatenate([zc, wc_im], axis=0)).astype(wci_ref.dtype)


def _s5_prep(a_re, a_im, log_dt, b_re, b_im, c_re, c_im):
    g, p = a_re.shape
    ar = a_re.reshape(g, 1, p)
    ai = a_im.reshape(g, 1, p)
    ld = jnp.broadcast_to(log_dt.reshape(g, 1, 1), (g, 1, p))
    arc = a_re.reshape(g, p, 1)
    aic = a_im.reshape(g, p, 1)
    ldc = jnp.broadcast_to(log_dt.reshape(g, 1, 1), (g, p, 1))
    btr = jnp.tile(jnp.swapaxes(b_re, 1, 2), (1, S5_SUB, 1))
    bti = jnp.tile(jnp.swapaxes(b_im, 1, 2), (1, S5_SUB, 1))
    ctr = jnp.tile(jnp.swapaxes(c_re, 1, 2), (1, 1, S5_SUB))
    cti = jnp.tile(jnp.swapaxes(c_im, 1, 2), (1, 1, S5_SUB))

    def gspec(shape):
        return pl.BlockSpec((None,) + shape, lambda i: (i,) + (0,) * len(shape))

    outs = [((S5_IN, S5_IN), BF16), ((S5_IN, 2 * S5_STATE), BF16), ((S5_IN, 2 * S5_STATE), BF16),
            ((2 * S5_STATE, S5_IN), BF16), ((2 * S5_STATE, S5_IN), BF16),
            ((1, S5_STATE), F32), ((1, S5_STATE), F32)]
    return pl.pallas_call(
        _s5_prep_body,
        grid=(g,),
        in_specs=[gspec((1, p)), gspec((1, p)), gspec((1, p)), gspec((p, 1)), gspec((p, 1)), gspec((p, 1)),
                  gspec((S5_IN, p)), gspec((S5_IN, p)), gspec((p, S5_IN)), gspec((p, S5_IN))],
        out_specs=[gspec(s) for s, _ in outs],
        out_shape=[jax.ShapeDtypeStruct((g,) + s, d) for s, d in outs],
        compiler_params=_compiler_params(("parallel",)),
        name="s5_prep",
    )(ar, ai, ld, arc, aic, ldc, btr, bti, ctr, cti)


def _s5_body(u_ref, toep_ref, wor_ref, woi_ref, wcr_ref, wci_ref, a16r_ref, a16i_ref, y_ref,
             lre_ref, lim_ref, sre_ref, sim_ref):
    nb = 8
    nsub = u_ref.shape[0] // nb
    u0 = u_ref[:, :S5_IN]
    u1 = u_ref[:, S5_IN:]
    lre_ref[...] = _dot(u0, wor_ref[0]) + _dot(u1, wor_ref[1])
    lim_ref[...] = _dot(u0, woi_ref[0]) + _dot(u1, woi_ref[1])
    a_re = a16r_ref[...]
    a_im = a16i_ref[...]

    def step(c, carry):
        s_re, s_im = carry
        r0 = pl.multiple_of(c * nb, nb)
        sre_ref[pl.ds(r0, nb), :] = s_re
        sim_ref[pl.ds(r0, nb), :] = s_im
        n_re = a_re * s_re - a_im * s_im + lre_ref[pl.ds(r0, nb), :]
        n_im = a_re * s_im + a_im * s_re + lim_ref[pl.ds(r0, nb), :]
        return n_re, n_im

    zero = jnp.zeros((nb, 2 * S5_STATE), F32)
    lax.fori_loop(0, nsub, step, (zero, zero))
    s_re = sre_ref[...].astype(BF16)
    s_im = sim_ref[...].astype(BF16)
    y_ref[:, :S5_IN] = _dot(u0, toep_ref[0]) + _dot(s_re, wcr_ref[0]) + _dot(s_im, wci_ref[0])
    y_ref[:, S5_IN:] = _dot(u1, toep_ref[1]) + _dot(s_re, wcr_ref[1]) + _dot(s_im, wci_ref[1])


def _s5_scan(u_t, toep, wor, woi, wcr, wci, a16r, a16i):
    npair, rows, _ = u_t.shape

    def pspec(shape):
        return pl.BlockSpec((None,) + shape, lambda i: (i,) + (0,) * len(shape))

    def gspec(shape):
        return pl.BlockSpec((2,) + shape, lambda i: (i,) + (0,) * len(shape))

    return pl.pallas_call(
        _s5_body,
        grid=(npair,),
        in_specs=[pspec((rows, 2 * S5_IN)), gspec((S5_IN, S5_IN)), gspec((S5_IN, 2 * S5_STATE)),
                  gspec((S5_IN, 2 * S5_STATE)), gspec((2 * S5_STATE, S5_IN)), gspec((2 * S5_STATE, S5_IN)),
                  pspec((1, 2 * S5_STATE)), pspec((1, 2 * S5_STATE))],
        out_specs=pspec((rows, 2 * S5_IN)),
        out_shape=jax.ShapeDtypeStruct((npair, rows, 2 * S5_IN), F32),
        scratch_shapes=[pltpu.VMEM((rows, 2 * S5_STATE), F32) for _ in range(4)],
        compiler_params=_compiler_params(("parallel",)),
        name="s5_scan",
    )(u_t, toep, wor, woi, wcr, wci, a16r, a16i)


OUT_TM = 512


def _outproj_body(h_ref, ya_ref, yb_ref, yc_ref, yd_ref, ud_ref, dsk_ref, wglu_ref, ng_ref, wo_ref, o_ref):
    yd = yd_ref[...] + dsk_ref[...] * ud_ref[...]
    yd = _gelu_tanh(yd)
    yd = yd * jax.nn.sigmoid(_dot(yd.astype(BF16), wglu_ref[...]))
    yd = _rms(yd, ng_ref[...]).astype(BF16)
    acc = _dot(ya_ref[...], wo_ref[0:W_GRP, :])
    acc = acc + _dot(yb_ref[...], wo_ref[W_GRP:2 * W_GRP, :])
    acc = acc + _dot(yc_ref[...], wo_ref[2 * W_GRP:3 * W_GRP, :])
    acc = acc + _dot(yd, wo_ref[3 * W_GRP:4 * W_GRP, :])
    o_ref[...] = h_ref[...] + acc


def _outproj(h, ya, yb, yc, yd, ud, dsk, wglu, ng, wo):
    m = h.shape[0]
    tok = pl.BlockSpec((OUT_TM, D_MODEL), lambda i: (i, 0))
    grp = pl.BlockSpec((OUT_TM, W_GRP), lambda i: (i, 0))
    return pl.pallas_call(
        _outproj_body,
        grid=(m // OUT_TM,),
        in_specs=[tok, grp, grp, grp, grp, grp, _const_spec(dsk.shape), _const_spec(wglu.shape),
                  _const_spec(ng.shape), _const_spec(wo.shape)],
        out_specs=tok,
        out_shape=jax.ShapeDtypeStruct(h.shape, F32),
        compiler_params=_compiler_params(("parallel",)),
        name="outproj",
    )(h, ya, yb, yc, yd, ud, dsk, wglu, ng, wo)


def _final_norm_body(h_ref, g_ref, o_ref):
    o_ref[...] = _rms(h_ref[...], g_ref[...])


def _final_norm(h, g):
    m = h.shape[0]
    tok = pl.BlockSpec((OUT_TM, D_MODEL), lambda i: (i, 0))
    return pl.pallas_call(
        _final_norm_body,
        grid=(m // OUT_TM,),
        in_specs=[tok, _const_spec((1, D_MODEL))],
        out_specs=tok,
        out_shape=jax.ShapeDtypeStruct(h.shape, F32),
        compiler_params=_compiler_params(("parallel",)),
        name="final_norm",
    )(h, g)


def _block_diag(w):
    s, i, j = w.shape
    eye = jnp.eye(s, dtype=w.dtype)
    return (w[:, :, None, :] * eye[:, None, :, None]).reshape(s * i, s * j)


def _small_row(vals, off):
    return jnp.zeros((1, SMALL_W), F32).at[0, off:off + vals.shape[0]].set(vals)


def _pack_w_in(w_in):
    sizes = (W_GRP, W_GRP, 3 * W_GRP, W_GRP, GDN_HEADS, GDN_HEADS, W_GRP, XBC_W, SSD_HEADS, W_GRP)
    offs = np.concatenate([[0], np.cumsum(sizes)])
    a_x, a_gate, b_qkv, b_z, b_beta, b_alpha, c_z, c_xbc, c_dt, d_u = (
        w_in[:, offs[i]:offs[i + 1]] for i in range(len(sizes)))
    small = jnp.zeros((w_in.shape[0], SMALL_W), w_in.dtype)
    small = small.at[:, BETA_OFF:BETA_OFF + GDN_HEADS].set(b_beta)
    small = small.at[:, ALPHA_OFF:ALPHA_OFF + GDN_HEADS].set(b_alpha)
    small = small.at[:, DT_OFF:DT_OFF + SSD_HEADS].set(c_dt)
    return jnp.concatenate([a_x, a_gate, b_qkv, b_z, c_z, c_xbc, d_u, small], axis=1)


def kernel(x, meta_tokens, ffn1_norm, ffn1_w_gate, ffn1_w_up, ffn1_w_down, mix_norm, w_in, w_out,
           lru_conv_w, lru_conv_b, lru_w_a, lru_b_a, lru_w_i, lru_b_i, lru_lambda, lru_norm,
           gdn_conv_w, gdn_a_log, gdn_dt_bias, gdn_norm,
           ssd_conv_w, ssd_conv_b, ssd_a_log, ssd_dt_bias, ssd_d, ssd_norm,
           s5_a_re, s5_a_im, s5_log_dt, s5_b_re, s5_b_im, s5_c_re, s5_c_im, s5_d, s5_w_glu, s5_norm,
           ffn2_norm, ffn2_w_gate, ffn2_w_up, ffn2_w_down, final_norm):
    bsz, seq, d = x.shape
    depth = w_in.shape[0]
    tp = LEAD_PAD + N_META + seq
    assert d == D_MODEL and tp % MIX_TT == 0 and tp % S5_SUB == 0 and bsz == 8
    m = bsz * tp
    assert m % FFN_TM == 0 and m % PROJ_TM == 0 and m % OUT_TM == 0
    nsub = tp // S5_SUB

    meta = jnp.broadcast_to(meta_tokens.astype(x.dtype)[None], (bsz, N_META, d))
    h = jnp.concatenate([jnp.zeros((bsz, LEAD_PAD, d), x.dtype), meta, x], axis=1).reshape(m, d)

    row = lambda v: v.reshape(1, -1).astype(F32)
    for l in range(depth):
        h = _ffn(h, row(ffn1_norm[l]), ffn1_w_gate[l].astype(BF16), ffn1_w_up[l].astype(BF16),
                 ffn1_w_down[l].astype(BF16))

        a_x, a_gate, b_qkv, b_z, c_z, c_xbc, d_u, small = _inproj(
            h, row(mix_norm[l]), _pack_w_in(w_in[l]).astype(BF16))
        seq3 = lambda t: t.reshape(bsz, tp, t.shape[-1])

        wai = jnp.concatenate([_block_diag(lru_w_a[l]), _block_diag(lru_w_i[l])], axis=1).astype(BF16)
        bai = jnp.concatenate([lru_b_a[l], lru_b_i[l]]).reshape(1, -1)
        y_a = _lru(seq3(a_x), seq3(a_gate), lru_conv_w[l], row(lru_conv_b[l]), wai, bai,
                   row(lru_lambda[l]), row(lru_norm[l]))

        y_b = _gdn(seq3(b_qkv), seq3(b_z), seq3(small), gdn_conv_w[l],
                   _small_row(gdn_a_log[l], ALPHA_OFF), _small_row(gdn_dt_bias[l], ALPHA_OFF),
                   row(gdn_norm[l]))

        y_c = _ssd(seq3(c_z), seq3(c_xbc), seq3(small), ssd_conv_w[l], row(ssd_conv_b[l]),
                   _small_row(ssd_a_log[l], DT_OFF), _small_row(ssd_dt_bias[l], DT_OFF),
                   row(jnp.repeat(ssd_d[l], SSD_HEAD_DIM)), row(ssd_norm[l]))

        toep, wor, woi, wcr, wci, a16r, a16i = _s5_prep(
            s5_a_re[l], s5_a_im[l], s5_log_dt[l], s5_b_re[l], s5_b_im[l], s5_c_re[l], s5_c_im[l])
        u_t = d_u.astype(BF16).reshape(bsz, nsub, S5_SUB, S5_PAIRS, 2, S5_GROUP_CH)
        u_t = u_t.transpose(3, 1, 0, 4, 2, 5).reshape(S5_PAIRS, nsub * bsz, 2 * S5_IN)
        y_t = _s5_scan(u_t, toep, wor, woi, wcr, wci,
                       a16r.reshape(S5_PAIRS, 1, 2 * S5_STATE), a16i.reshape(S5_PAIRS, 1, 2 * S5_STATE))
        y_d = y_t.reshape(S5_PAIRS, nsub, bsz, 2, S5_SUB, S5_GROUP_CH)
        y_d = y_d.transpose(2, 1, 4, 0, 3, 5).reshape(m, W_GRP)

        flat = lambda t: t.reshape(m, W_GRP)
        h = _outproj(h, flat(y_a), flat(y_b), flat(y_c), y_d, d_u, row(s5_d[l]),
                     s5_w_glu[l].astype(BF16), row(s5_norm[l]), w_out[l].astype(BF16))

        h = _ffn(h, row(ffn2_norm[l]), ffn2_w_gate[l].astype(BF16), ffn2_w_up[l].astype(BF16),
                 ffn2_w_down[l].astype(BF16))

    out = _final_norm(h, row(final_norm))
    return out.reshape(bsz, tp, d)[:, LEAD_PAD + N_META:]
```

```python
import numpy as np
import jax
import jax.numpy as jnp
from jax import lax
from jax.experimental import pallas as pl
from jax.experimental.pallas import tpu as pltpu

F32 = jnp.float32
BF16 = jnp.bfloat16
HI = lax.Precision.HIGHEST

D_MODEL = 1024
N_META = 16
W_GRP = 512
CONV_K = 4
D_FF = 2816
EPS = 1e-6
CHUNK = 64
LEAD_PAD = CHUNK - N_META
LRU_C = 8.0
GDN_HEADS = 4
GDN_HEAD_DIM = 128
SSD_HEADS = 8
SSD_HEAD_DIM = 64
SSD_GROUPS = 2
SSD_STATE = 128
XBC_W = W_GRP + 2 * SSD_GROUPS * SSD_STATE
S5_GROUP_CH = 16
S5_GROUPS = 32
S5_STATE = 64

LANES = 128
SUBLANES = 8
SMALL_W = LANES
BETA_OFF, ALPHA_OFF, DT_OFF = 0, 4, 8
VMEM_LIMIT = 56 * 1024 * 1024

S5_SUB = SUBLANES
S5_OCT = W_GRP // LANES
S5_OCT_GROUPS = S5_GROUPS // S5_OCT
S5_VEC = S5_SUB * LANES
S5_NST = S5_OCT_GROUPS * S5_STATE

NN = (((1,), (0,)), ((), ()))
NT = (((1,), (1,)), ((), ()))


def _dot(a, b, dims=NN, prec=None):
    return lax.dot_general(a, b, dims, precision=prec, preferred_element_type=F32)


def _bdot(a, b, dims=NN):
    return lax.dot_general(a.astype(BF16), b.astype(BF16), dims, preferred_element_type=F32)


def _rms(x, g):
    return x * lax.rsqrt(jnp.mean(x * x, axis=-1, keepdims=True) + EPS) * g


def _silu(x):
    return x * jax.nn.sigmoid(x)


def _gelu_tanh(x):
    return 0.5 * x * (1.0 + jnp.tanh(np.sqrt(2.0 / np.pi).astype(np.float32) * (x + 0.044715 * (x * x * x))))


def _softplus(x):
    return jnp.maximum(x, 0.0) + jnp.log1p(jnp.exp(-jnp.abs(x)))


def _row_iota(shape):
    return lax.broadcasted_iota(jnp.int32, shape, 0)


def _col_iota(shape):
    return lax.broadcasted_iota(jnp.int32, shape, 1)


def _shift_rows(x, prev8, k):
    xs = pltpu.roll(x, k, axis=0)
    ps = pltpu.roll(prev8, k, axis=0)
    top = jnp.where(_row_iota(ps.shape) < k, ps, xs[:SUBLANES])
    return jnp.concatenate([top, xs[SUBLANES:]], axis=0)


def _causal_conv4(x, prev8, w_ref):
    y = x * w_ref[CONV_K - 1:CONV_K, :]
    for k in range(1, CONV_K):
        y = y + _shift_rows(x, prev8, k) * w_ref[CONV_K - 1 - k:CONV_K - k, :]
    return y


def _compiler_params(sem):
    return pltpu.CompilerParams(dimension_semantics=sem, vmem_limit_bytes=VMEM_LIMIT)


def _const_spec(shape):
    nd = len(shape)
    return pl.BlockSpec(shape, lambda *_: (0,) * nd)


def _valid_rows(row0, nrows, ncols):
    return _row_iota((nrows, ncols)) + row0 >= LEAD_PAD


def _seq_spec(tt, ncols):
    return pl.BlockSpec((None, tt, ncols), lambda b, t: (b, t, 0))


def _oct_spec(tt):
    return pl.BlockSpec((S5_OCT, tt // S5_SUB, None, S5_SUB, LANES), lambda b, t: (0, t, b, 0, 0))


FFN_TM = 512
FFN_FC = 256


def _ffn_body(h_ref, g_ref, wg_ref, wu_ref, wd_ref, o_ref):
    h = h_ref[...]
    xn = _rms(h, g_ref[...]).astype(BF16)
    acc = jnp.zeros_like(h)
    for c in range(D_FF // FFN_FC):
        sl = slice(c * FFN_FC, (c + 1) * FFN_FC)
        gt = _dot(xn, wg_ref[:, sl])
        up = _dot(xn, wu_ref[:, sl])
        acc = acc + _dot((_silu(gt) * up).astype(BF16), wd_ref[sl, :])
    o_ref[...] = h + 0.5 * acc


def _ffn(h, g, wg, wu, wd):
    m = h.shape[0]
    tok = pl.BlockSpec((FFN_TM, D_MODEL), lambda i: (i, 0))
    return pl.pallas_call(
        _ffn_body,
        grid=(m // FFN_TM,),
        in_specs=[tok, _const_spec((1, D_MODEL)), _const_spec((D_MODEL, D_FF)),
                  _const_spec((D_MODEL, D_FF)), _const_spec((D_FF, D_MODEL))],
        out_specs=tok,
        out_shape=jax.ShapeDtypeStruct(h.shape, F32),
        compiler_params=_compiler_params(("parallel",)),
        name="ffn",
    )(h, g, wg, wu, wd)


PROJ_TM = 704
PROJ_SLABS = (W_GRP, W_GRP, 3 * W_GRP, W_GRP, W_GRP, XBC_W, SMALL_W)
PROJ_W = sum(PROJ_SLABS) + W_GRP


def _inproj_body(h_ref, g_ref, w_ref, *out_refs):
    xn = _rms(h_ref[...], g_ref[...]).astype(BF16)
    off = 0
    for o_ref in out_refs[:-1]:
        n = o_ref.shape[-1]
        o_ref[...] = _dot(xn, w_ref[:, off:off + n])
        off += n
    u = _dot(xn, w_ref[:, off:off + W_GRP])
    u_ref = out_refs[-1]
    for q in range(S5_OCT):
        u_ref[q] = u[:, q * LANES:(q + 1) * LANES].reshape(PROJ_TM // S5_SUB, S5_SUB, LANES)


def _inproj(h, g, w):
    b, tp, _ = h.shape
    return pl.pallas_call(
        _inproj_body,
        grid=(b, tp // PROJ_TM),
        in_specs=[_seq_spec(PROJ_TM, D_MODEL), _const_spec((1, D_MODEL)), _const_spec((D_MODEL, PROJ_W))],
        out_specs=[_seq_spec(PROJ_TM, n) for n in PROJ_SLABS] + [_oct_spec(PROJ_TM)],
        out_shape=[jax.ShapeDtypeStruct((b, tp, n), F32) for n in PROJ_SLABS]
        + [jax.ShapeDtypeStruct((S5_OCT, tp // S5_SUB, b, S5_SUB, LANES), F32)],
        compiler_params=_compiler_params(("parallel", "parallel")),
        name="inproj",
    )(h, g, w)


LRU_TT = 704


def _lru_body(ux_ref, ug_ref, cw_ref, cb_ref, wai_ref, bai_ref, lam_ref, ng_ref, o_ref,
              prev_ref, hst_ref):
    t_idx = pl.program_id(1)

    @pl.when(t_idx == 0)
    def _():
        prev_ref[...] = jnp.zeros_like(prev_ref)
        hst_ref[...] = jnp.zeros_like(hst_ref)

    neg_c_sp = -LRU_C * _softplus(-lam_ref[...])

    def chunk(c, carry):
        r0 = pl.multiple_of(c * CHUNK, CHUNK)
        x = ux_ref[pl.ds(r0, CHUNK), :]
        xc = _causal_conv4(x, prev_ref[...], cw_ref) + cb_ref[...]
        prev_ref[...] = x[CHUNK - SUBLANES:]
        gates = _dot(xc.astype(BF16), wai_ref[...]) + bai_ref[...]
        r = jax.nn.sigmoid(gates[:, :W_GRP])
        ig = jax.nn.sigmoid(gates[:, W_GRP:])
        log_a = neg_c_sp * r
        a = jnp.exp(log_a)
        valid = _valid_rows(t_idx * LRU_TT + r0, CHUNK, W_GRP)
        bv = jnp.sqrt(jnp.tanh(-log_a) * (1.0 + a * a)) * (ig * xc)
        bv = jnp.where(valid, bv, 0.0)
        rows = _row_iota((CHUNK, W_GRP))
        s = 1
        while s < CHUNK:
            a_sh = jnp.where(rows < s, 1.0, pltpu.roll(a, s, axis=0))
            b_sh = jnp.where(rows < s, 0.0, pltpu.roll(bv, s, axis=0))
            bv = a * b_sh + bv
            a = a * a_sh
            s *= 2
        hcur = bv + a * hst_ref[0:1, :]
        hst_ref[...] = jnp.broadcast_to(hcur[CHUNK - 1:CHUNK, :], hst_ref.shape)
        y = _gelu_tanh(ug_ref[pl.ds(r0, CHUNK), :]) * hcur
        y = _rms(y, ng_ref[...])
        y = jnp.where(valid, y, 0.0)
        o_ref[pl.ds(r0, CHUNK), :] = y.astype(o_ref.dtype)
        return carry

    lax.fori_loop(0, LRU_TT // CHUNK, chunk, 0)


def _lru(ux, ug, cw, cb, wai, bai, lam, ng):
    b, tp, _ = ux.shape
    return pl.pallas_call(
        _lru_body,
        grid=(b, tp // LRU_TT),
        in_specs=[_seq_spec(LRU_TT, W_GRP), _seq_spec(LRU_TT, W_GRP), _const_spec(cw.shape),
                  _const_spec(cb.shape), _const_spec(wai.shape), _const_spec(bai.shape),
                  _const_spec(lam.shape), _const_spec(ng.shape)],
        out_specs=_seq_spec(LRU_TT, W_GRP),
        out_shape=jax.ShapeDtypeStruct((b, tp, W_GRP), BF16),
        scratch_shapes=[pltpu.VMEM((SUBLANES, W_GRP), F32), pltpu.VMEM((SUBLANES, W_GRP), F32)],
        compiler_params=_compiler_params(("parallel", "arbitrary")),
        name="lru",
    )(ux, ug, cw, cb, wai, bai, lam, ng)


CM_NC = 3
CM_TT = CM_NC * CHUNK


def _tri_masks():
    r = _row_iota((CHUNK, CHUNK))
    c = _col_iota((CHUNK, CHUNK))
    return r >= c, r > c


def _chunk_cumsum(x):
    incl, _ = _tri_masks()
    tril = incl.astype(F32)
    triu = (_row_iota((CHUNK, CHUNK)) <= _col_iota((CHUNK, CHUNK))).astype(F32)
    return _dot(tril, x, prec=HI), _dot(x.T, triu, prec=HI)


def _gdn_body(qkv_ref, z_ref, sm_ref, cw_ref, alog_ref, dtb_ref, ng_ref, o_ref, prev_ref, s_ref):
    t_idx = pl.program_id(1)

    @pl.when(t_idx == 0)
    def _():
        prev_ref[...] = jnp.zeros_like(prev_ref)
        s_ref[...] = jnp.zeros_like(s_ref)

    incl, strict = _tri_masks()
    hd = GDN_HEAD_DIM
    x = qkv_ref[...]
    qkv = _silu(_causal_conv4(x, prev_ref[...], cw_ref))
    prev_ref[...] = x[CM_TT - SUBLANES:]
    sm = sm_ref[...]
    beta_all = jax.nn.sigmoid(sm)
    g_all = -jnp.exp(alog_ref[...]) * _softplus(sm + dtb_ref[...])
    g_all = jnp.where(_valid_rows(t_idx * CM_TT, CM_TT, SMALL_W), g_all, 0.0)
    valid = _valid_rows(t_idx * CM_TT, CM_TT, hd)
    cums = [_chunk_cumsum(g_all[c * CHUNK:(c + 1) * CHUNK]) for c in range(CM_NC)]

    heads = range(GDN_HEADS)
    pairs = [(h, c) for h in heads for c in range(CM_NC)]
    rs = [slice(c * CHUNK, (c + 1) * CHUNK) for c in range(CM_NC)]
    qs, ks, kbs, vbs = [], [], [], []
    for h in heads:
        q = qkv[:, h * hd:(h + 1) * hd]
        k = qkv[:, W_GRP + h * hd:W_GRP + (h + 1) * hd]
        v = qkv[:, 2 * W_GRP + h * hd:2 * W_GRP + (h + 1) * hd]
        qs.append(q * lax.rsqrt(jnp.sum(q * q, axis=-1, keepdims=True) + EPS) * (hd ** -0.5))
        k = k * lax.rsqrt(jnp.sum(k * k, axis=-1, keepdims=True) + EPS)
        beta = beta_all[:, BETA_OFF + h:BETA_OFF + h + 1]
        ks.append(k)
        kbs.append(k * beta)
        vbs.append(v * beta)
    gcol = {(h, c): cums[c][0][:, ALPHA_OFF + h:ALPHA_OFF + h + 1] for h, c in pairs}
    grow = {(h, c): cums[c][1][ALPHA_OFF + h:ALPHA_OFF + h + 1, :] for h, c in pairs}
    decay = {p: jnp.exp(jnp.where(incl, gcol[p] - grow[p], -jnp.inf)) for p in pairs}
    eg = {p: jnp.exp(gcol[p]) for p in pairs}
    g_last = {p: gcol[p][CHUNK - 1:CHUNK, :] for p in pairs}

    kk = {(h, c): _bdot(kbs[h][rs[c]], ks[h][rs[c]], NT) for h, c in pairs}
    qk = {(h, c): _bdot(qs[h][rs[c]], ks[h][rs[c]], NT) for h, c in pairs}
    nmat = {p: jnp.where(strict, kk[p] * decay[p], 0.0) for p in pairs}
    attn = {p: qk[p] * decay[p] for p in pairs}
    eye = (_row_iota((CHUNK, CHUNK)) == _col_iota((CHUNK, CHUNK))).astype(F32)
    tinv = {p: eye - nmat[p] for p in pairs}
    npow = {p: _bdot(nmat[p], nmat[p]) for p in pairs}
    span = 2
    while True:
        tinv = {p: tinv[p] + _bdot(tinv[p], npow[p]) for p in pairs}
        span *= 2
        if span >= CHUNK:
            break
        npow = {p: _bdot(npow[p], npow[p]) for p in pairs}
    u = {(h, c): _bdot(tinv[(h, c)], vbs[h][rs[c]]) for h, c in pairs}
    w = {(h, c): _bdot(tinv[(h, c)], kbs[h][rs[c]] * eg[(h, c)]) for h, c in pairs}
    q_dec = {(h, c): qs[h][rs[c]] * eg[(h, c)] for h, c in pairs}
    k_dec_t = {(h, c): (ks[h][rs[c]] * jnp.exp(g_last[(h, c)] - gcol[(h, c)])).T for h, c in pairs}

    s = [s_ref[h] for h in heads]
    outs = [[None] * CM_NC for _ in heads]
    for c in range(CM_NC):
        ws = [_bdot(w[(h, c)], s[h]) for h in heads]
        qd = [_bdot(q_dec[(h, c)], s[h]) for h in heads]
        v_new = [u[(h, c)] - ws[h] for h in heads]
        av = [_bdot(attn[(h, c)], v_new[h]) for h in heads]
        kv = [_bdot(k_dec_t[(h, c)], v_new[h]) for h in heads]
        for h in heads:
            outs[h][c] = qd[h] + av[h]
            s[h] = s[h] * jnp.exp(g_last[(h, c)]) + kv[h]
    for h in heads:
        s_ref[h] = s[h]
        o = jnp.concatenate(outs[h], axis=0)
        o = _rms(o, ng_ref[...]) * _silu(z_ref[:, h * hd:(h + 1) * hd])
        o = jnp.where(valid, o, 0.0)
        o_ref[:, h * hd:(h + 1) * hd] = o.astype(o_ref.dtype)


def _gdn(qkv, z, sm, cw, alog, dtb, ng):
    b, tp, _ = qkv.shape
    return pl.pallas_call(
        _gdn_body,
        grid=(b, tp // CM_TT),
        in_specs=[_seq_spec(CM_TT, 3 * W_GRP), _seq_spec(CM_TT, W_GRP), _seq_spec(CM_TT, SMALL_W),
                  _const_spec(cw.shape), _const_spec(alog.shape), _const_spec(dtb.shape),
                  _const_spec(ng.shape)],
        out_specs=_seq_spec(CM_TT, W_GRP),
        out_shape=jax.ShapeDtypeStruct((b, tp, W_GRP), BF16),
        scratch_shapes=[pltpu.VMEM((SUBLANES, 3 * W_GRP), F32),
                        pltpu.VMEM((GDN_HEADS, GDN_HEAD_DIM, GDN_HEAD_DIM), F32)],
        compiler_params=_compiler_params(("parallel", "arbitrary")),
        name="gdn",
    )(qkv, z, sm, cw, alog, dtb, ng)


def _ssd_body(z_ref, xbc_ref, sm_ref, cw_ref, cb_ref, alog_ref, dtb_ref, dsk_ref, ng_ref, o_ref,
              prev_ref, s_ref):
    t_idx = pl.program_id(1)

    @pl.when(t_idx == 0)
    def _():
        prev_ref[...] = jnp.zeros_like(prev_ref)
        s_ref[...] = jnp.zeros_like(s_ref)

    incl, _ = _tri_masks()
    hp = SSD_HEAD_DIM
    gw = W_GRP // SSD_GROUPS
    hpg = SSD_HEADS // SSD_GROUPS
    expand = (_row_iota((SMALL_W, W_GRP)) - DT_OFF
              == _col_iota((SMALL_W, W_GRP)) >> (hp.bit_length() - 1)).astype(F32)

    x = xbc_ref[...]
    xbc = _silu(_causal_conv4(x, prev_ref[...], cw_ref) + cb_ref[...])
    prev_ref[...] = x[CM_TT - SUBLANES:]
    xs = xbc[:, :W_GRP]
    dt_all = _softplus(sm_ref[...] + dtb_ref[...])
    dt_all = jnp.where(_valid_rows(t_idx * CM_TT, CM_TT, SMALL_W), dt_all, 0.0)
    adt_all = dt_all * -jnp.exp(alog_ref[...])
    x_in = xs * _dot(dt_all, expand, prec=HI)
    cums = [_chunk_cumsum(adt_all[c * CHUNK:(c + 1) * CHUNK]) for c in range(CM_NC)]

    ys = [[None] * SSD_HEADS for _ in range(CM_NC)]
    for g in range(SSD_GROUPS):
        b_all = xbc[:, W_GRP + g * SSD_STATE:W_GRP + (g + 1) * SSD_STATE]
        c_all = xbc[:, W_GRP + (SSD_GROUPS + g) * SSD_STATE:W_GRP + (SSD_GROUPS + g + 1) * SSD_STATE]
        cbs = [_bdot(c_all[c * CHUNK:(c + 1) * CHUNK], b_all[c * CHUNK:(c + 1) * CHUNK], NT)
               for c in range(CM_NC)]
        for e in range(hpg):
            h = g * hpg + e
            s = s_ref[h]
            for c in range(CM_NC):
                rs = slice(c * CHUNK, (c + 1) * CHUNK)
                acol = cums[c][0][:, DT_OFF + h:DT_OFF + h + 1]
                arow = cums[c][1][DT_OFF + h:DT_OFF + h + 1, :]
                a_last = acol[CHUNK - 1:CHUNK, :]
                lmat = jnp.exp(jnp.where(incl, acol - arow, -jnp.inf))
                xh = x_in[rs, h * hp:(h + 1) * hp]
                ys[c][h] = _bdot(cbs[c] * lmat, xh) + _bdot(c_all[rs], s, NT) * jnp.exp(acol)
                st = _bdot((xh * jnp.exp(a_last - acol)).T, b_all[rs])
                s = s * jnp.exp(a_last) + st
            s_ref[h] = s
    y = jnp.concatenate([jnp.concatenate(ys[c], axis=1) for c in range(CM_NC)], axis=0)
    y = (y + dsk_ref[...] * xs) * _silu(z_ref[...])
    y = jnp.concatenate(
        [_rms(y[:, g * gw:(g + 1) * gw], ng_ref[:, g * gw:(g + 1) * gw]) for g in range(SSD_GROUPS)],
        axis=1)
    y = jnp.where(_valid_rows(t_idx * CM_TT, CM_TT, W_GRP), y, 0.0)
    o_ref[...] = y.astype(o_ref.dtype)


def _ssd(z, xbc, sm, cw, cb, alog, dtb, dsk, ng):
    b, tp, _ = z.shape
    return pl.pallas_call(
        _ssd_body,
        grid=(b, tp // CM_TT),
        in_specs=[_seq_spec(CM_TT, W_GRP), _seq_spec(CM_TT, XBC_W), _seq_spec(CM_TT, SMALL_W),
                  _const_spec(cw.shape), _const_spec(cb.shape), _const_spec(alog.shape),
                  _const_spec(dtb.shape), _const_spec(dsk.shape), _const_spec(ng.shape)],
        out_specs=_seq_spec(CM_TT, W_GRP),
        out_shape=jax.ShapeDtypeStruct((b, tp, W_GRP), BF16),
        scratch_shapes=[pltpu.VMEM((SUBLANES, XBC_W), F32),
                        pltpu.VMEM((SSD_HEADS, SSD_HEAD_DIM, SSD_STATE), F32)],
        compiler_params=_compiler_params(("parallel", "arbitrary")),
        name="ssd",
    )(z, xbc, sm, cw, cb, alog, dtb, dsk, ng)


def _s5_discretise(a_re, a_im, log_dt):
    lam_re = jnp.minimum(a_re, -1e-4)
    lam_im = a_im
    dt = jnp.exp(log_dt)
    mag = jnp.exp(dt * lam_re)
    ab_re = mag * jnp.cos(dt * lam_im)
    ab_im = mag * jnp.sin(dt * lam_im)
    den = lam_re * lam_re + lam_im * lam_im
    f_re = ((ab_re - 1.0) * lam_re + ab_im * lam_im) / den
    f_im = (ab_im * lam_re - (ab_re - 1.0) * lam_im) / den
    return ab_re, ab_im, f_re, f_im


def _s5_prep_body(ar_ref, ai_ref, ld_ref, arc_ref, aic_ref, ldc_ref, btr_ref, bti_ref, ctr_ref, cti_ref,
                  w1_ref, wc_ref, apow_ref):
    ab_re, ab_im, f_re, f_im = _s5_discretise(ar_ref[...], ai_ref[...], ld_ref[...])
    abc_re, abc_im, _, _ = _s5_discretise(arc_ref[...], aic_ref[...], ldc_ref[...])
    bb_re = f_re * btr_ref[...] - f_im * bti_ref[...]
    bb_im = f_re * bti_ref[...] + f_im * btr_ref[...]
    ct_re = ctr_ref[...]
    ct_im = cti_ref[...]

    pows = [(jnp.ones_like(ab_re), jnp.zeros_like(ab_re))]
    for _ in range(S5_SUB):
        p_re, p_im = pows[-1]
        pows.append((p_re * ab_re - p_im * ab_im, p_re * ab_im + p_im * ab_re))
    x_re = jnp.concatenate([bb_re * pows[S5_SUB - 1 - s][0] - bb_im * pows[S5_SUB - 1 - s][1]
                            for s in range(S5_SUB)], axis=0)
    x_im = jnp.concatenate([bb_re * pows[S5_SUB - 1 - s][1] + bb_im * pows[S5_SUB - 1 - s][0]
                            for s in range(S5_SUB)], axis=0)
    taps = _dot(x_re, ct_re, prec=HI) - _dot(x_im, ct_im, prec=HI)
    cols = []
    for l in range(S5_SUB):
        up = (S5_SUB - 1 - l) * LANES
        cols.append(taps[up:] if up == 0 else
                    jnp.concatenate([taps[up:], jnp.zeros((up, LANES), F32)], axis=0))
    w1_ref[...] = jnp.concatenate(cols + [x_re, x_im], axis=1).astype(w1_ref.dtype)
    apow_ref[...] = jnp.concatenate(pows[S5_SUB], axis=1)

    q_re, q_im = abc_re, abc_im
    wc_re, wc_im = [], []
    for l in range(S5_SUB):
        wc_re.append(ct_re * q_re - ct_im * q_im)
        wc_im.append(-(ct_re * q_im + ct_im * q_re))
        q_re, q_im = q_re * abc_re - q_im * abc_im, q_re * abc_im + q_im * abc_re
    wc_ref[...] = jnp.concatenate([jnp.concatenate(wc_re, axis=1), jnp.concatenate(wc_im, axis=1)],
                                  axis=0).astype(wc_ref.dtype)


def _block_diag(w):
    s, i, j = w.shape
    eye = jnp.eye(s, dtype=w.dtype)
    return (w[:, :, None, :] * eye[:, None, :, None]).reshape(s * i, s * j)


def _s5_prep(a_re, a_im, log_dt, b_re, b_im, c_re, c_im):
    row = lambda v: v.reshape(S5_OCT, 1, S5_NST)
    col = lambda v: v.reshape(S5_OCT, S5_NST, 1)
    ld = jnp.repeat(log_dt, S5_STATE)
    octs = lambda w: jnp.stack([_block_diag(jnp.swapaxes(w, 1, 2)[q * S5_OCT_GROUPS:(q + 1) * S5_OCT_GROUPS])
                                for q in range(S5_OCT)])

    def ospec(shape):
        return pl.BlockSpec((None,) + shape, lambda i: (i,) + (0,) * len(shape))

    outs = [((S5_VEC, S5_VEC + 2 * S5_NST), BF16), ((2 * S5_NST, S5_VEC), BF16), ((1, 2 * S5_NST), F32)]
    return pl.pallas_call(
        _s5_prep_body,
        grid=(S5_OCT,),
        in_specs=[ospec((1, S5_NST))] * 3 + [ospec((S5_NST, 1))] * 3
        + [ospec((LANES, S5_NST))] * 2 + [ospec((S5_NST, LANES))] * 2,
        out_specs=[ospec(s) for s, _ in outs],
        out_shape=[jax.ShapeDtypeStruct((S5_OCT,) + s, d) for s, d in outs],
        compiler_params=_compiler_params(("parallel",)),
        name="s5_prep",
    )(row(a_re), row(a_im), row(ld), col(a_re), col(a_im), col(ld),
      octs(b_re), octs(b_im), octs(c_re), octs(c_im))


S5_RT = 704


def _s5_body(u_ref, w1_ref, wc_ref, apow_ref, y_ref, loc_ref, sin_ref, st_ref):
    @pl.when(pl.program_id(1) == 0)
    def _():
        st_ref[...] = jnp.zeros_like(st_ref)

    nb = st_ref.shape[0]
    z = _dot(u_ref[...].astype(BF16), w1_ref[...])
    loc_ref[...] = z[:, S5_VEC:]
    a_re = apow_ref[:, :S5_NST]
    a_im = apow_ref[:, S5_NST:]

    def step(c, carry):
        s_re, s_im = carry
        r0 = pl.multiple_of(c * nb, nb)
        sin_ref[pl.ds(r0, nb), :S5_NST] = s_re
        sin_ref[pl.ds(r0, nb), S5_NST:] = s_im
        n_re = a_re * s_re - a_im * s_im + loc_ref[pl.ds(r0, nb), :S5_NST]
        n_im = a_re * s_im + a_im * s_re + loc_ref[pl.ds(r0, nb), S5_NST:]
        return n_re, n_im

    s_re, s_im = lax.fori_loop(0, S5_RT // nb, step, (st_ref[:, :S5_NST], st_ref[:, S5_NST:]))
    st_ref[:, :S5_NST] = s_re
    st_ref[:, S5_NST:] = s_im
    y_ref[...] = z[:, :S5_VEC] + _dot(sin_ref[...].astype(BF16), wc_ref[...])


def _s5_scan(u, w1, wc, apow, nb):
    noct, rows, _ = u.shape

    def ospec(shape):
        return pl.BlockSpec((None,) + shape, lambda q, r: (q,) + (0,) * len(shape))

    tile = pl.BlockSpec((None, S5_RT, S5_VEC), lambda q, r: (q, r, 0))
    return pl.pallas_call(
        _s5_body,
        grid=(noct, rows // S5_RT),
        in_specs=[tile, ospec(w1.shape[1:]), ospec(wc.shape[1:]), ospec(apow.shape[1:])],
        out_specs=tile,
        out_shape=jax.ShapeDtypeStruct(u.shape, F32),
        scratch_shapes=[pltpu.VMEM((S5_RT, 2 * S5_NST), F32), pltpu.VMEM((S5_RT, 2 * S5_NST), F32),
                        pltpu.VMEM((nb, 2 * S5_NST), F32)],
        compiler_params=_compiler_params(("parallel", "arbitrary")),
        name="s5_scan",
    )(u, w1, wc, apow)


OUT_TM = 704


def _outproj_body(h_ref, ya_ref, yb_ref, yc_ref, yd_ref, ud_ref, dsk_ref, wglu_ref, ng_ref, wo_ref, o_ref):
    tok = lambda ref: jnp.concatenate([ref[q].reshape(OUT_TM, LANES) for q in range(S5_OCT)], axis=1)
    yd = tok(yd_ref) + dsk_ref[...] * tok(ud_ref)
    yd = _gelu_tanh(yd)
    yd = yd * jax.nn.sigmoid(_dot(yd.astype(BF16), wglu_ref[...]))
    yd = _rms(yd, ng_ref[...]).astype(BF16)
    acc = _dot(ya_ref[...], wo_ref[0:W_GRP, :])
    acc = acc + _dot(yb_ref[...], wo_ref[W_GRP:2 * W_GRP, :])
    acc = acc + _dot(yc_ref[...], wo_ref[2 * W_GRP:3 * W_GRP, :])
    acc = acc + _dot(yd, wo_ref[3 * W_GRP:4 * W_GRP, :])
    o_ref[...] = h_ref[...] + acc


def _outproj(h, ya, yb, yc, yd, ud, dsk, wglu, ng, wo):
    b, tp, _ = h.shape
    tok = _seq_spec(OUT_TM, D_MODEL)
    grp = _seq_spec(OUT_TM, W_GRP)
    return pl.pallas_call(
        _outproj_body,
        grid=(b, tp // OUT_TM),
        in_specs=[tok, grp, grp, grp, _oct_spec(OUT_TM), _oct_spec(OUT_TM), _const_spec(dsk.shape),
                  _const_spec(wglu.shape), _const_spec(ng.shape), _const_spec(wo.shape)],
        out_specs=tok,
        out_shape=jax.ShapeDtypeStruct(h.shape, F32),
        compiler_params=_compiler_params(("parallel", "parallel")),
        name="outproj",
    )(h, ya, yb, yc, yd, ud, dsk, wglu, ng, wo)


def _final_norm_body(h_ref, g_ref, o_ref):
    o_ref[...] = _rms(h_ref[...], g_ref[...])


def _final_norm(h, g):
    m = h.shape[0]
    tok = pl.BlockSpec((FFN_TM, D_MODEL), lambda i: (i, 0))
    return pl.pallas_call(
        _final_norm_body,
        grid=(m // FFN_TM,),
        in_specs=[tok, _const_spec((1, D_MODEL))],
        out_specs=tok,
        out_shape=jax.ShapeDtypeStruct(h.shape, F32),
        compiler_params=_compiler_params(("parallel",)),
        name="final_norm",
    )(h, g)


def _small_row(vals, off):
    return jnp.zeros((1, SMALL_W), F32).at[0, off:off + vals.shape[0]].set(vals)


def _pack_w_in(w_in):
    sizes = (W_GRP, W_GRP, 3 * W_GRP, W_GRP, GDN_HEADS, GDN_HEADS, W_GRP, XBC_W, SSD_HEADS, W_GRP)
    offs = np.concatenate([[0], np.cumsum(sizes)])
    a_x, a_gate, b_qkv, b_z, b_beta, b_alpha, c_z, c_xbc, c_dt, d_u = (
        w_in[:, offs[i]:offs[i + 1]] for i in range(len(sizes)))
    small = jnp.zeros((w_in.shape[0], SMALL_W), w_in.dtype)
    small = small.at[:, BETA_OFF:BETA_OFF + GDN_HEADS].set(b_beta)
    small = small.at[:, ALPHA_OFF:ALPHA_OFF + GDN_HEADS].set(b_alpha)
    small = small.at[:, DT_OFF:DT_OFF + SSD_HEADS].set(c_dt)
    return jnp.concatenate([a_x, a_gate, b_qkv, b_z, c_z, c_xbc, small, d_u], axis=1)


def kernel(x, meta_tokens, ffn1_norm, ffn1_w_gate, ffn1_w_up, ffn1_w_down, mix_norm, w_in, w_out,
           lru_conv_w, lru_conv_b, lru_w_a, lru_b_a, lru_w_i, lru_b_i, lru_lambda, lru_norm,
           gdn_conv_w, gdn_a_log, gdn_dt_bias, gdn_norm,
           ssd_conv_w, ssd_conv_b, ssd_a_log, ssd_dt_bias, ssd_d, ssd_norm,
           s5_a_re, s5_a_im, s5_log_dt, s5_b_re, s5_b_im, s5_c_re, s5_c_im, s5_d, s5_w_glu, s5_norm,
           ffn2_norm, ffn2_w_gate, ffn2_w_up, ffn2_w_down, final_norm):
    bsz, seq, d = x.shape
    depth = w_in.shape[0]
    tp = LEAD_PAD + N_META + seq
    m = bsz * tp
    nsub = tp // S5_SUB
    assert d == D_MODEL and bsz == SUBLANES and m % FFN_TM == 0
    assert all(tp % t == 0 for t in (PROJ_TM, LRU_TT, CM_TT, OUT_TM)) and (nsub * bsz) % S5_RT == 0

    meta = jnp.broadcast_to(meta_tokens.astype(x.dtype)[None], (bsz, N_META, d))
    h = jnp.concatenate([jnp.zeros((bsz, LEAD_PAD, d), x.dtype), meta, x], axis=1)

    row = lambda v: v.reshape(1, -1).astype(F32)
    flat = lambda t: t.reshape(m, d)
    seq3 = lambda t: t.reshape(bsz, tp, d)
    for l in range(depth):
        h = seq3(_ffn(flat(h), row(ffn1_norm[l]), ffn1_w_gate[l].astype(BF16), ffn1_w_up[l].astype(BF16),
                      ffn1_w_down[l].astype(BF16)))

        a_x, a_gate, b_qkv, b_z, c_z, c_xbc, small, d_u = _inproj(
            h, row(mix_norm[l]), _pack_w_in(w_in[l]).astype(BF16))

        wai = jnp.concatenate([_block_diag(lru_w_a[l]), _block_diag(lru_w_i[l])], axis=1).astype(BF16)
        bai = jnp.concatenate([lru_b_a[l], lru_b_i[l]]).reshape(1, -1)
        y_a = _lru(a_x, a_gate, lru_conv_w[l], row(lru_conv_b[l]), wai, bai,
                   row(lru_lambda[l]), row(lru_norm[l]))

        y_b = _gdn(b_qkv, b_z, small, gdn_conv_w[l],
                   _small_row(gdn_a_log[l], ALPHA_OFF), _small_row(gdn_dt_bias[l], ALPHA_OFF),
                   row(gdn_norm[l]))

        y_c = _ssd(c_z, c_xbc, small, ssd_conv_w[l], row(ssd_conv_b[l]),
                   _small_row(ssd_a_log[l], DT_OFF), _small_row(ssd_dt_bias[l], DT_OFF),
                   row(jnp.repeat(ssd_d[l], SSD_HEAD_DIM)), row(ssd_norm[l]))

        w1, wc, apow = _s5_prep(s5_a_re[l], s5_a_im[l], s5_log_dt[l], s5_b_re[l], s5_b_im[l],
                                s5_c_re[l], s5_c_im[l])
        y_d = _s5_scan(d_u.reshape(S5_OCT, nsub * bsz, S5_VEC), w1, wc, apow, bsz).reshape(d_u.shape)

        h = _outproj(h, y_a, y_b, y_c, y_d, d_u, row(s5_d[l]),
                     s5_w_glu[l].astype(BF16), row(s5_norm[l]), w_out[l].astype(BF16))

        h = seq3(_ffn(flat(h), row(ffn2_norm[l]), ffn2_w_gate[l].astype(BF16), ffn2_w_up[l].astype(BF16),
                      ffn2_w_down[l].astype(BF16)))

    out = _final_norm(flat(h), row(final_norm))
    return out.reshape(bsz, tp, d)[:, LEAD_PAD + N_META:]
```

```python
import numpy as np
import jax
import jax.numpy as jnp
from jax import lax
from jax.experimental import pallas as pl
from jax.experimental.pallas import tpu as pltpu

F32 = jnp.float32
BF16 = jnp.bfloat16
HI = lax.Precision.HIGHEST

D_MODEL = 1024
N_META = 16
W_GRP = 512
CONV_K = 4
D_FF = 2816
EPS = 1e-6
CHUNK = 64
LEAD_PAD = CHUNK - N_META
LRU_C = 8.0
GDN_HEADS = 4
GDN_HEAD_DIM = 128
SSD_HEADS = 8
SSD_HEAD_DIM = 64
SSD_GROUPS = 2
SSD_STATE = 128
XBC_W = W_GRP + 2 * SSD_GROUPS * SSD_STATE
S5_GROUP_CH = 16
S5_GROUPS = 32
S5_STATE = 64

LANES = 128
SUBLANES = 8
SMALL_W = LANES
BETA_OFF, ALPHA_OFF, DT_OFF = 0, 4, 8
VMEM_LIMIT = 56 * 1024 * 1024

S5_SUB = SUBLANES
S5_OCT = W_GRP // LANES
S5_OCT_GROUPS = S5_GROUPS // S5_OCT
S5_VEC = S5_SUB * LANES
S5_NST = S5_OCT_GROUPS * S5_STATE

NN = (((1,), (0,)), ((), ()))
NT = (((1,), (1,)), ((), ()))


def _dot(a, b, dims=NN, prec=None):
    return lax.dot_general(a, b, dims, precision=prec, preferred_element_type=F32)


def _bdot(a, b, dims=NN):
    return lax.dot_general(a.astype(BF16), b.astype(BF16), dims, preferred_element_type=F32)


def _rms(x, g):
    return x * lax.rsqrt(jnp.mean(x * x, axis=-1, keepdims=True) + EPS) * g


def _silu(x):
    return x * jax.nn.sigmoid(x)


def _gelu_tanh(x):
    return 0.5 * x * (1.0 + jnp.tanh(np.sqrt(2.0 / np.pi).astype(np.float32) * (x + 0.044715 * (x * x * x))))


def _softplus(x):
    return jnp.maximum(x, 0.0) + jnp.log1p(jnp.exp(-jnp.abs(x)))


def _row_iota(shape):
    return lax.broadcasted_iota(jnp.int32, shape, 0)


def _col_iota(shape):
    return lax.broadcasted_iota(jnp.int32, shape, 1)


def _shift_rows(x, prev8, k):
    xs = pltpu.roll(x, k, axis=0)
    ps = pltpu.roll(prev8, k, axis=0)
    top = jnp.where(_row_iota(ps.shape) < k, ps, xs[:SUBLANES])
    return jnp.concatenate([top, xs[SUBLANES:]], axis=0)


def _causal_conv4(x, prev8, w_ref):
    y = x * w_ref[CONV_K - 1:CONV_K, :]
    for k in range(1, CONV_K):
        y = y + _shift_rows(x, prev8, k) * w_ref[CONV_K - 1 - k:CONV_K - k, :]
    return y


def _compiler_params(sem):
    return pltpu.CompilerParams(dimension_semantics=sem, vmem_limit_bytes=VMEM_LIMIT)


def _layer_spec(arr, l):
    shape = arr.shape[1:]
    return pl.BlockSpec((None,) + shape, lambda *_: (l,) + (0,) * len(shape),
                        pipeline_mode=pl.Buffered(1))


def _valid_rows(row0, nrows, ncols):
    return _row_iota((nrows, ncols)) + row0 >= LEAD_PAD


def _seq_spec(tt, ncols):
    return pl.BlockSpec((None, tt, ncols), lambda b, t: (b, t, 0))


def _oct_spec(tt):
    return pl.BlockSpec((S5_OCT, tt // S5_SUB, None, S5_SUB, LANES), lambda b, t: (0, t, b, 0, 0))


FFN_TM = 1056
FFN_FC = 256


def _ffn_body(h_ref, g_ref, wg_ref, wu_ref, wd_ref, o_ref):
    h = h_ref[...]
    xn = _rms(h, g_ref[...]).astype(BF16)
    acc = jnp.zeros_like(h)
    for c in range(D_FF // FFN_FC):
        sl = slice(c * FFN_FC, (c + 1) * FFN_FC)
        gt = _dot(xn, wg_ref[:, sl])
        up = _dot(xn, wu_ref[:, sl])
        acc = acc + _dot((_silu(gt) * up).astype(BF16), wd_ref[sl, :])
    o_ref[...] = h + 0.5 * acc


def _ffn(h, l, g, wg, wu, wd):
    m = h.shape[0]
    tok = pl.BlockSpec((FFN_TM, D_MODEL), lambda i: (i, 0))
    return pl.pallas_call(
        _ffn_body,
        grid=(m // FFN_TM,),
        in_specs=[tok] + [_layer_spec(p, l) for p in (g, wg, wu, wd)],
        out_specs=tok,
        out_shape=jax.ShapeDtypeStruct(h.shape, F32),
        compiler_params=_compiler_params(("parallel",)),
        name="ffn",
    )(h, g, wg, wu, wd)


PROJ_TM = 704
PROJ_SLABS = (W_GRP, W_GRP, 3 * W_GRP, W_GRP, W_GRP, XBC_W, SMALL_W)
PROJ_W = sum(PROJ_SLABS) + W_GRP


def _inproj_body(h_ref, g_ref, w_ref, *out_refs):
    xn = _rms(h_ref[...], g_ref[...]).astype(BF16)
    off = 0
    for o_ref in out_refs[:-1]:
        n = o_ref.shape[-1]
        o_ref[...] = _dot(xn, w_ref[:, off:off + n])
        off += n
    u = _dot(xn, w_ref[:, off:off + W_GRP])
    u_ref = out_refs[-1]
    for q in range(S5_OCT):
        u_ref[q] = u[:, q * LANES:(q + 1) * LANES].reshape(PROJ_TM // S5_SUB, S5_SUB, LANES)


def _inproj(h, l, g, w):
    b, tp, _ = h.shape
    return pl.pallas_call(
        _inproj_body,
        grid=(b, tp // PROJ_TM),
        in_specs=[_seq_spec(PROJ_TM, D_MODEL), _layer_spec(g, l), _layer_spec(w, l)],
        out_specs=[_seq_spec(PROJ_TM, n) for n in PROJ_SLABS] + [_oct_spec(PROJ_TM)],
        out_shape=[jax.ShapeDtypeStruct((b, tp, n), F32) for n in PROJ_SLABS]
        + [jax.ShapeDtypeStruct((S5_OCT, tp // S5_SUB, b, S5_SUB, LANES), F32)],
        compiler_params=_compiler_params(("parallel", "parallel")),
        name="inproj",
    )(h, g, w)


LRU_TT = 704


def _lru_body(ux_ref, ug_ref, cw_ref, cb_ref, wai_ref, bai_ref, lam_ref, ng_ref, o_ref,
              prev_ref, hst_ref):
    t_idx = pl.program_id(1)

    @pl.when(t_idx == 0)
    def _():
        prev_ref[...] = jnp.zeros_like(prev_ref)
        hst_ref[...] = jnp.zeros_like(hst_ref)

    neg_c_sp = -LRU_C * _softplus(-lam_ref[...])

    def chunk(c, carry):
        r0 = pl.multiple_of(c * CHUNK, CHUNK)
        x = ux_ref[pl.ds(r0, CHUNK), :]
        xc = _causal_conv4(x, prev_ref[...], cw_ref) + cb_ref[...]
        prev_ref[...] = x[CHUNK - SUBLANES:]
        gates = _dot(xc.astype(BF16), wai_ref[...]) + bai_ref[...]
        r = jax.nn.sigmoid(gates[:, :W_GRP])
        ig = jax.nn.sigmoid(gates[:, W_GRP:])
        log_a = neg_c_sp * r
        a = jnp.exp(log_a)
        valid = _valid_rows(t_idx * LRU_TT + r0, CHUNK, W_GRP)
        bv = jnp.sqrt(jnp.tanh(-log_a) * (1.0 + a * a)) * (ig * xc)
        bv = jnp.where(valid, bv, 0.0)
        sub = _row_iota((CHUNK, W_GRP)) & (SUBLANES - 1)
        s = 1
        while s < SUBLANES:
            a_sh = jnp.where(sub < s, 1.0, pltpu.roll(a, s, axis=0))
            b_sh = jnp.where(sub < s, 0.0, pltpu.roll(bv, s, axis=0))
            bv = a * b_sh + bv
            a = a * a_sh
            s *= 2
        carry_row = hst_ref[0:1, :]
        groups = []
        for g in range(CHUNK // SUBLANES):
            rows = slice(g * SUBLANES, (g + 1) * SUBLANES)
            hg = bv[rows] + a[rows] * carry_row
            carry_row = hg[SUBLANES - 1:SUBLANES, :]
            groups.append(hg)
        hcur = jnp.concatenate(groups, axis=0)
        hst_ref[...] = jnp.broadcast_to(carry_row, hst_ref.shape)
        y = _gelu_tanh(ug_ref[pl.ds(r0, CHUNK), :]) * hcur
        y = _rms(y, ng_ref[...])
        y = jnp.where(valid, y, 0.0)
        o_ref[pl.ds(r0, CHUNK), :] = y.astype(o_ref.dtype)
        return carry

    lax.fori_loop(0, LRU_TT // CHUNK, chunk, 0)


def _lru(ux, ug, l, cw, cb, wai, bai, lam, ng):
    b, tp, _ = ux.shape
    return pl.pallas_call(
        _lru_body,
        grid=(b, tp // LRU_TT),
        in_specs=[_seq_spec(LRU_TT, W_GRP), _seq_spec(LRU_TT, W_GRP)]
        + [_layer_spec(p, l) for p in (cw, cb, wai, bai, lam, ng)],
        out_specs=_seq_spec(LRU_TT, W_GRP),
        out_shape=jax.ShapeDtypeStruct((b, tp, W_GRP), BF16),
        scratch_shapes=[pltpu.VMEM((SUBLANES, W_GRP), F32), pltpu.VMEM((SUBLANES, W_GRP), F32)],
        compiler_params=_compiler_params(("parallel", "arbitrary")),
        name="lru",
    )(ux, ug, cw, cb, wai, bai, lam, ng)


CM_NC = 3
CM_TT = CM_NC * CHUNK


def _tri_masks():
    r = _row_iota((CHUNK, CHUNK))
    c = _col_iota((CHUNK, CHUNK))
    return r >= c, r > c


def _chunk_cumsum(x):
    incl, _ = _tri_masks()
    tril = incl.astype(F32)
    triu = (_row_iota((CHUNK, CHUNK)) <= _col_iota((CHUNK, CHUNK))).astype(F32)
    return _dot(tril, x, prec=HI), _dot(x.T, triu, prec=HI)


def _gdn_body(qkv_ref, z_ref, sm_ref, cw_ref, alog_ref, dtb_ref, ng_ref, o_ref, prev_ref, s_ref):
    t_idx = pl.program_id(1)

    @pl.when(t_idx == 0)
    def _():
        prev_ref[...] = jnp.zeros_like(prev_ref)
        s_ref[...] = jnp.zeros_like(s_ref)

    incl, strict = _tri_masks()
    hd = GDN_HEAD_DIM
    x = qkv_ref[...]
    qkv = _silu(_causal_conv4(x, prev_ref[...], cw_ref))
    prev_ref[...] = x[CM_TT - SUBLANES:]
    sm = sm_ref[...]
    beta_all = jax.nn.sigmoid(sm)
    g_all = -jnp.exp(alog_ref[...]) * _softplus(sm + dtb_ref[...])
    g_all = jnp.where(_valid_rows(t_idx * CM_TT, CM_TT, SMALL_W), g_all, 0.0)
    valid = _valid_rows(t_idx * CM_TT, CM_TT, hd)
    cums = [_chunk_cumsum(g_all[c * CHUNK:(c + 1) * CHUNK]) for c in range(CM_NC)]

    heads = range(GDN_HEADS)
    pairs = [(h, c) for h in heads for c in range(CM_NC)]
    rs = [slice(c * CHUNK, (c + 1) * CHUNK) for c in range(CM_NC)]
    qs, ks, kbs, vbs = [], [], [], []
    for h in heads:
        q = qkv[:, h * hd:(h + 1) * hd]
        k = qkv[:, W_GRP + h * hd:W_GRP + (h + 1) * hd]
        v = qkv[:, 2 * W_GRP + h * hd:2 * W_GRP + (h + 1) * hd]
        qs.append(q * lax.rsqrt(jnp.sum(q * q, axis=-1, keepdims=True) + EPS) * (hd ** -0.5))
        k = k * lax.rsqrt(jnp.sum(k * k, axis=-1, keepdims=True) + EPS)
        beta = beta_all[:, BETA_OFF + h:BETA_OFF + h + 1]
        ks.append(k)
        kbs.append(k * beta)
        vbs.append(v * beta)
    gcol = {(h, c): cums[c][0][:, ALPHA_OFF + h:ALPHA_OFF + h + 1] for h, c in pairs}
    grow = {(h, c): cums[c][1][ALPHA_OFF + h:ALPHA_OFF + h + 1, :] for h, c in pairs}
    decay = {p: jnp.exp(jnp.where(incl, gcol[p] - grow[p], -jnp.inf)) for p in pairs}
    eg = {p: jnp.exp(gcol[p]) for p in pairs}
    g_last = {p: gcol[p][CHUNK - 1:CHUNK, :] for p in pairs}

    kk = {(h, c): _bdot(kbs[h][rs[c]], ks[h][rs[c]], NT) for h, c in pairs}
    qk = {(h, c): _bdot(qs[h][rs[c]], ks[h][rs[c]], NT) for h, c in pairs}
    nmat = {p: jnp.where(strict, kk[p] * decay[p], 0.0) for p in pairs}
    attn = {p: qk[p] * decay[p] for p in pairs}
    eye = (_row_iota((CHUNK, CHUNK)) == _col_iota((CHUNK, CHUNK))).astype(F32)
    tinv = {p: eye - nmat[p] for p in pairs}
    npow = {p: _bdot(nmat[p], nmat[p]) for p in pairs}
    span = 2
    while True:
        tinv = {p: tinv[p] + _bdot(tinv[p], npow[p]) for p in pairs}
        span *= 2
        if span >= CHUNK:
            break
        npow = {p: _bdot(npow[p], npow[p]) for p in pairs}
    u = {(h, c): _bdot(tinv[(h, c)], vbs[h][rs[c]]) for h, c in pairs}
    w = {(h, c): _bdot(tinv[(h, c)], kbs[h][rs[c]] * eg[(h, c)]) for h, c in pairs}
    q_dec = {(h, c): qs[h][rs[c]] * eg[(h, c)] for h, c in pairs}
    k_dec_t = {(h, c): (ks[h][rs[c]] * jnp.exp(g_last[(h, c)] - gcol[(h, c)])).T for h, c in pairs}

    s = [s_ref[h] for h in heads]
    outs = [[None] * CM_NC for _ in heads]
    for c in range(CM_NC):
        ws = [_bdot(w[(h, c)], s[h]) for h in heads]
        qd = [_bdot(q_dec[(h, c)], s[h]) for h in heads]
        v_new = [u[(h, c)] - ws[h] for h in heads]
        av = [_bdot(attn[(h, c)], v_new[h]) for h in heads]
        kv = [_bdot(k_dec_t[(h, c)], v_new[h]) for h in heads]
        for h in heads:
            outs[h][c] = qd[h] + av[h]
            s[h] = s[h] * jnp.exp(g_last[(h, c)]) + kv[h]
    for h in heads:
        s_ref[h] = s[h]
        o = jnp.concatenate(outs[h], axis=0)
        o = _rms(o, ng_ref[...]) * _silu(z_ref[:, h * hd:(h + 1) * hd])
        o = jnp.where(valid, o, 0.0)
        o_ref[:, h * hd:(h + 1) * hd] = o.astype(o_ref.dtype)


def _gdn(qkv, z, sm, l, cw, alog, dtb, ng):
    b, tp, _ = qkv.shape
    return pl.pallas_call(
        _gdn_body,
        grid=(b, tp // CM_TT),
        in_specs=[_seq_spec(CM_TT, 3 * W_GRP), _seq_spec(CM_TT, W_GRP), _seq_spec(CM_TT, SMALL_W)]
        + [_layer_spec(p, l) for p in (cw, alog, dtb, ng)],
        out_specs=_seq_spec(CM_TT, W_GRP),
        out_shape=jax.ShapeDtypeStruct((b, tp, W_GRP), BF16),
        scratch_shapes=[pltpu.VMEM((SUBLANES, 3 * W_GRP), F32),
                        pltpu.VMEM((GDN_HEADS, GDN_HEAD_DIM, GDN_HEAD_DIM), F32)],
        compiler_params=_compiler_params(("parallel", "arbitrary")),
        name="gdn",
    )(qkv, z, sm, cw, alog, dtb, ng)


SSD_PAIRS = SSD_HEADS // 2


def _ssd_body(z_ref, xbc_ref, sm_ref, cw_ref, cb_ref, alog_ref, dtb_ref, dsk_ref, ng_ref, o_ref,
              prev_ref, s_ref):
    t_idx = pl.program_id(1)

    @pl.when(t_idx == 0)
    def _():
        prev_ref[...] = jnp.zeros_like(prev_ref)
        s_ref[...] = jnp.zeros_like(s_ref)

    hp = SSD_HEAD_DIM
    gw = W_GRP // SSD_GROUPS
    ppg = SSD_PAIRS // SSD_GROUPS
    expand = (_row_iota((SMALL_W, W_GRP)) - DT_OFF
              == _col_iota((SMALL_W, W_GRP)) >> (hp.bit_length() - 1)).astype(F32)
    lane = _col_iota((CHUNK, LANES))
    lo_half = lane < hp
    incl2 = _row_iota((CHUNK, LANES)) >= (lane & (hp - 1))
    lo_rows = _row_iota((LANES, 1)) < hp

    x = xbc_ref[...]
    xbc = _silu(_causal_conv4(x, prev_ref[...], cw_ref) + cb_ref[...])
    prev_ref[...] = x[CM_TT - SUBLANES:]
    xs = xbc[:, :W_GRP]
    dt_all = _softplus(sm_ref[...] + dtb_ref[...])
    dt_all = jnp.where(_valid_rows(t_idx * CM_TT, CM_TT, SMALL_W), dt_all, 0.0)
    adt_all = dt_all * -jnp.exp(alog_ref[...])
    x_in = xs * _dot(dt_all, expand, prec=HI)
    cums = [_chunk_cumsum(adt_all[c * CHUNK:(c + 1) * CHUNK]) for c in range(CM_NC)]
    rs = [slice(c * CHUNK, (c + 1) * CHUNK) for c in range(CM_NC)]
    pairs = [(j, c) for j in range(SSD_PAIRS) for c in range(CM_NC)]
    grp = lambda j: j // ppg

    def head_cols(j, c):
        cum, cum_t = cums[c]
        h0 = DT_OFF + 2 * j
        acol = jnp.where(lo_half, cum[:, h0:h0 + 1], cum[:, h0 + 1:h0 + 2])
        arow = jnp.concatenate([cum_t[h0:h0 + 1, :], cum_t[h0 + 1:h0 + 2, :]], axis=1)
        return acol, arow

    b_all = [xbc[:, W_GRP + g * SSD_STATE:W_GRP + (g + 1) * SSD_STATE] for g in range(SSD_GROUPS)]
    c_all = [xbc[:, W_GRP + (SSD_GROUPS + g) * SSD_STATE:W_GRP + (SSD_GROUPS + g + 1) * SSD_STATE]
             for g in range(SSD_GROUPS)]
    cols = {p: head_cols(*p) for p in pairs}
    a_last = {p: cols[p][0][CHUNK - 1:CHUNK, :] for p in pairs}
    xp = {(j, c): x_in[rs[c], j * LANES:(j + 1) * LANES] for j, c in pairs}

    cb = {(g, c): _bdot(c_all[g][rs[c]], b_all[g][rs[c]], NT) for g in range(SSD_GROUPS) for c in range(CM_NC)}
    cb2 = {k: jnp.concatenate([v, v], axis=1) for k, v in cb.items()}
    lm = {p: cb2[(grp(p[0]), p[1])] * jnp.exp(jnp.where(incl2, cols[p][0] - cols[p][1], -jnp.inf))
          for p in pairs}
    x_bd = {p: jnp.concatenate([jnp.where(lo_half, xp[p], 0.0), jnp.where(lo_half, 0.0, xp[p])], axis=0)
            for p in pairs}
    y_diag = {p: _bdot(lm[p], x_bd[p]) for p in pairs}
    st = {(j, c): _bdot((xp[(j, c)] * jnp.exp(a_last[(j, c)] - cols[(j, c)][0])).T, b_all[grp(j)][rs[c]])
          for j, c in pairs}
    s_in = {}
    for j in range(SSD_PAIRS):
        s = s_ref[j]
        for c in range(CM_NC):
            s_in[(j, c)] = s
            e_last = jnp.exp(a_last[(j, c)])
            s = s * jnp.where(lo_rows, e_last[:, 0:1], e_last[:, hp:hp + 1]) + st[(j, c)]
        s_ref[j] = s
    y = {(j, c): y_diag[(j, c)] + _bdot(c_all[grp(j)][rs[c]], s_in[(j, c)], NT) * jnp.exp(cols[(j, c)][0])
         for j, c in pairs}
    y = jnp.concatenate([jnp.concatenate([y[(j, c)] for j in range(SSD_PAIRS)], axis=1)
                         for c in range(CM_NC)], axis=0)
    y = (y + dsk_ref[...] * xs) * _silu(z_ref[...])
    y = jnp.concatenate(
        [_rms(y[:, g * gw:(g + 1) * gw], ng_ref[:, g * gw:(g + 1) * gw]) for g in range(SSD_GROUPS)],
        axis=1)
    y = jnp.where(_valid_rows(t_idx * CM_TT, CM_TT, W_GRP), y, 0.0)
    o_ref[...] = y.astype(o_ref.dtype)


def _ssd(z, xbc, sm, l, cw, cb, alog, dtb, dsk, ng):
    b, tp, _ = z.shape
    return pl.pallas_call(
        _ssd_body,
        grid=(b, tp // CM_TT),
        in_specs=[_seq_spec(CM_TT, W_GRP), _seq_spec(CM_TT, XBC_W), _seq_spec(CM_TT, SMALL_W)]
        + [_layer_spec(p, l) for p in (cw, cb, alog, dtb, dsk, ng)],
        out_specs=_seq_spec(CM_TT, W_GRP),
        out_shape=jax.ShapeDtypeStruct((b, tp, W_GRP), BF16),
        scratch_shapes=[pltpu.VMEM((SUBLANES, XBC_W), F32),
                        pltpu.VMEM((SSD_PAIRS, 2 * SSD_HEAD_DIM, SSD_STATE), F32)],
        compiler_params=_compiler_params(("parallel", "arbitrary")),
        name="ssd",
    )(z, xbc, sm, cw, cb, alog, dtb, dsk, ng)


def _s5_discretise(a_re, a_im, log_dt):
    lam_re = jnp.minimum(a_re, -1e-4)
    lam_im = a_im
    dt = jnp.exp(log_dt)
    mag = jnp.exp(dt * lam_re)
    ab_re = mag * jnp.cos(dt * lam_im)
    ab_im = mag * jnp.sin(dt * lam_im)
    den = lam_re * lam_re + lam_im * lam_im
    f_re = ((ab_re - 1.0) * lam_re + ab_im * lam_im) / den
    f_im = (ab_im * lam_re - (ab_re - 1.0) * lam_im) / den
    return ab_re, ab_im, f_re, f_im


def _s5_prep_body(ar_ref, ai_ref, ld_ref, arc_ref, aic_ref, ldc_ref, btr_ref, bti_ref, ctr_ref, cti_ref,
                  w1_ref, wc_ref, apow_ref):
    ab_re, ab_im, f_re, f_im = _s5_discretise(ar_ref[...], ai_ref[...], ld_ref[...])
    abc_re, abc_im, _, _ = _s5_discretise(arc_ref[...], aic_ref[...], ldc_ref[...])
    bb_re = f_re * btr_ref[...] - f_im * bti_ref[...]
    bb_im = f_re * bti_ref[...] + f_im * btr_ref[...]
    ct_re = ctr_ref[...]
    ct_im = cti_ref[...]

    pows = [(jnp.ones_like(ab_re), jnp.zeros_like(ab_re))]
    for _ in range(S5_SUB):
        p_re, p_im = pows[-1]
        pows.append((p_re * ab_re - p_im * ab_im, p_re * ab_im + p_im * ab_re))
    x_re = jnp.concatenate([bb_re * pows[S5_SUB - 1 - s][0] - bb_im * pows[S5_SUB - 1 - s][1]
                            for s in range(S5_SUB)], axis=0)
    x_im = jnp.concatenate([bb_re * pows[S5_SUB - 1 - s][1] + bb_im * pows[S5_SUB - 1 - s][0]
                            for s in range(S5_SUB)], axis=0)
    taps = _dot(x_re, ct_re, prec=HI) - _dot(x_im, ct_im, prec=HI)
    cols = []
    for l in range(S5_SUB):
        up = (S5_SUB - 1 - l) * LANES
        cols.append(taps[up:] if up == 0 else
                    jnp.concatenate([taps[up:], jnp.zeros((up, LANES), F32)], axis=0))
    w1_ref[...] = jnp.concatenate(cols + [x_re, x_im], axis=1).astype(w1_ref.dtype)
    apow_ref[...] = jnp.concatenate(pows[S5_SUB], axis=1)

    q_re, q_im = abc_re, abc_im
    wc_re, wc_im = [], []
    for l in range(S5_SUB):
        wc_re.append(ct_re * q_re - ct_im * q_im)
        wc_im.append(-(ct_re * q_im + ct_im * q_re))
        q_re, q_im = q_re * abc_re - q_im * abc_im, q_re * abc_im + q_im * abc_re
    wc_ref[...] = jnp.concatenate([jnp.concatenate(wc_re, axis=1), jnp.concatenate(wc_im, axis=1)],
                                  axis=0).astype(wc_ref.dtype)


def _block_diag(w):
    s, i, j = w.shape[-3:]
    eye = jnp.eye(s, dtype=w.dtype)
    return (w[..., :, :, None, :] * eye[:, None, :, None]).reshape(w.shape[:-3] + (s * i, s * j))


def _s5_prep(a_re, a_im, log_dt, b_re, b_im, c_re, c_im):
    depth = a_re.shape[0]
    row = lambda v: v.reshape(depth, S5_OCT, 1, S5_NST)
    col = lambda v: v.reshape(depth, S5_OCT, S5_NST, 1)
    ld = jnp.repeat(log_dt, S5_STATE, axis=1)
    octs = lambda w: _block_diag(jnp.swapaxes(w, 2, 3).reshape(
        (depth, S5_OCT, S5_OCT_GROUPS) + (w.shape[3], w.shape[2])))

    def ospec(shape):
        return pl.BlockSpec((None, None) + shape, lambda d, q: (d, q) + (0,) * len(shape))

    outs = [((S5_VEC, S5_VEC + 2 * S5_NST), BF16), ((2 * S5_NST, S5_VEC), BF16), ((1, 2 * S5_NST), F32)]
    return pl.pallas_call(
        _s5_prep_body,
        grid=(depth, S5_OCT),
        in_specs=[ospec((1, S5_NST))] * 3 + [ospec((S5_NST, 1))] * 3
        + [ospec((LANES, S5_NST))] * 2 + [ospec((S5_NST, LANES))] * 2,
        out_specs=[ospec(s) for s, _ in outs],
        out_shape=[jax.ShapeDtypeStruct((depth, S5_OCT) + s, d) for s, d in outs],
        compiler_params=_compiler_params(("parallel", "parallel")),
        name="s5_prep",
    )(row(a_re), row(a_im), row(ld), col(a_re), col(a_im), col(ld),
      octs(b_re), octs(b_im), octs(c_re), octs(c_im))


S5_RT = 704


def _s5_body(u_ref, w1_ref, wc_ref, apow_ref, y_ref, loc_ref, sin_ref, st_ref):
    @pl.when(pl.program_id(1) == 0)
    def _():
        st_ref[...] = jnp.zeros_like(st_ref)

    nb = st_ref.shape[0]
    u = jnp.concatenate([u_ref[pl.ds(s, S5_RT, stride=S5_SUB), :] for s in range(S5_SUB)], axis=1)
    z = _dot(u.astype(BF16), w1_ref[...])
    loc_ref[...] = z[:, S5_VEC:]
    a_re = apow_ref[:, :S5_NST]
    a_im = apow_ref[:, S5_NST:]

    def step(c, carry):
        s_re, s_im = carry
        r0 = pl.multiple_of(c * nb, nb)
        sin_ref[pl.ds(r0, nb), :S5_NST] = s_re
        sin_ref[pl.ds(r0, nb), S5_NST:] = s_im
        n_re = a_re * s_re - a_im * s_im + loc_ref[pl.ds(r0, nb), :S5_NST]
        n_im = a_re * s_im + a_im * s_re + loc_ref[pl.ds(r0, nb), S5_NST:]
        return n_re, n_im

    s_re, s_im = lax.fori_loop(0, S5_RT // nb, step, (st_ref[:, :S5_NST], st_ref[:, S5_NST:]))
    st_ref[:, :S5_NST] = s_re
    st_ref[:, S5_NST:] = s_im
    y = z[:, :S5_VEC] + _dot(sin_ref[...].astype(BF16), wc_ref[...])
    for l in range(S5_SUB):
        y_ref[pl.ds(l, S5_RT, stride=S5_SUB), :] = y[:, l * LANES:(l + 1) * LANES]


def _s5_scan(u, l, w1, wc, apow, nb):
    noct, rows, _ = u.shape

    def wspec(arr):
        shape = arr.shape[2:]
        return pl.BlockSpec((None, None) + shape, lambda q, r: (l, q) + (0,) * len(shape))

    tile = pl.BlockSpec((None, S5_RT * S5_SUB, LANES), lambda q, r: (q, r, 0))
    return pl.pallas_call(
        _s5_body,
        grid=(noct, rows // (S5_RT * S5_SUB)),
        in_specs=[tile, wspec(w1), wspec(wc), wspec(apow)],
        out_specs=tile,
        out_shape=jax.ShapeDtypeStruct(u.shape, F32),
        scratch_shapes=[pltpu.VMEM((S5_RT, 2 * S5_NST), F32), pltpu.VMEM((S5_RT, 2 * S5_NST), F32),
                        pltpu.VMEM((nb, 2 * S5_NST), F32)],
        compiler_params=_compiler_params(("parallel", "arbitrary")),
        name="s5_scan",
    )(u, w1, wc, apow)


OUT_TM = 704


def _outproj_body(h_ref, ya_ref, yb_ref, yc_ref, yd_ref, ud_ref, dsk_ref, wglu_ref, ng_ref, wo_ref, o_ref):
    tok = lambda ref: jnp.concatenate([ref[q].reshape(OUT_TM, LANES) for q in range(S5_OCT)], axis=1)
    yd = tok(yd_ref) + dsk_ref[...] * tok(ud_ref)
    yd = _gelu_tanh(yd)
    yd = yd * jax.nn.sigmoid(_dot(yd.astype(BF16), wglu_ref[...]))
    yd = _rms(yd, ng_ref[...]).astype(BF16)
    acc = _dot(ya_ref[...], wo_ref[0:W_GRP, :])
    acc = acc + _dot(yb_ref[...], wo_ref[W_GRP:2 * W_GRP, :])
    acc = acc + _dot(yc_ref[...], wo_ref[2 * W_GRP:3 * W_GRP, :])
    acc = acc + _dot(yd, wo_ref[3 * W_GRP:4 * W_GRP, :])
    o_ref[...] = h_ref[...] + acc


def _outproj(h, ya, yb, yc, yd, ud, l, dsk, wglu, ng, wo):
    b, tp, _ = h.shape
    tok = _seq_spec(OUT_TM, D_MODEL)
    grp = _seq_spec(OUT_TM, W_GRP)
    return pl.pallas_call(
        _outproj_body,
        grid=(b, tp // OUT_TM),
        in_specs=[tok, grp, grp, grp, _oct_spec(OUT_TM), _oct_spec(OUT_TM)]
        + [_layer_spec(p, l) for p in (dsk, wglu, ng, wo)],
        out_specs=tok,
        out_shape=jax.ShapeDtypeStruct(h.shape, F32),
        compiler_params=_compiler_params(("parallel", "parallel")),
        name="outproj",
    )(h, ya, yb, yc, yd, ud, dsk, wglu, ng, wo)


def _final_norm_body(h_ref, g_ref, o_ref):
    o_ref[...] = _rms(h_ref[...], g_ref[...])


def _final_norm(h, g):
    m = h.shape[0]
    tok = pl.BlockSpec((FFN_TM, D_MODEL), lambda i: (i, 0))
    return pl.pallas_call(
        _final_norm_body,
        grid=(m // FFN_TM,),
        in_specs=[tok, pl.BlockSpec((1, D_MODEL), lambda i: (0, 0))],
        out_specs=tok,
        out_shape=jax.ShapeDtypeStruct(h.shape, F32),
        compiler_params=_compiler_params(("parallel",)),
        name="final_norm",
    )(h, g)


def _small_rows(vals, off):
    depth, n = vals.shape
    return jnp.zeros((depth, 1, SMALL_W), F32).at[:, 0, off:off + n].set(vals)


def _pack_w_in(w_in):
    sizes = (W_GRP, W_GRP, 3 * W_GRP, W_GRP, GDN_HEADS, GDN_HEADS, W_GRP, XBC_W, SSD_HEADS, W_GRP)
    offs = np.concatenate([[0], np.cumsum(sizes)])
    a_x, a_gate, b_qkv, b_z, b_beta, b_alpha, c_z, c_xbc, c_dt, d_u = (
        w_in[..., offs[i]:offs[i + 1]] for i in range(len(sizes)))
    pad = lambda n: jnp.zeros(w_in.shape[:-1] + (n,), w_in.dtype)
    small = jnp.concatenate([b_beta, b_alpha, c_dt, pad(SMALL_W - DT_OFF - SSD_HEADS)], axis=-1)
    return jnp.concatenate([a_x, a_gate, b_qkv, b_z, c_z, c_xbc, small, d_u], axis=-1)


def kernel(x, meta_tokens, ffn1_norm, ffn1_w_gate, ffn1_w_up, ffn1_w_down, mix_norm, w_in, w_out,
           lru_conv_w, lru_conv_b, lru_w_a, lru_b_a, lru_w_i, lru_b_i, lru_lambda, lru_norm,
           gdn_conv_w, gdn_a_log, gdn_dt_bias, gdn_norm,
           ssd_conv_w, ssd_conv_b, ssd_a_log, ssd_dt_bias, ssd_d, ssd_norm,
           s5_a_re, s5_a_im, s5_log_dt, s5_b_re, s5_b_im, s5_c_re, s5_c_im, s5_d, s5_w_glu, s5_norm,
           ffn2_norm, ffn2_w_gate, ffn2_w_up, ffn2_w_down, final_norm):
    bsz, seq, d = x.shape
    depth = w_in.shape[0]
    tp = LEAD_PAD + N_META + seq
    m = bsz * tp
    nsub = tp // S5_SUB
    assert d == D_MODEL and bsz == SUBLANES and m % FFN_TM == 0
    assert all(tp % t == 0 for t in (PROJ_TM, LRU_TT, CM_TT, OUT_TM)) and (nsub * bsz) % S5_RT == 0
    assert (BETA_OFF, ALPHA_OFF, DT_OFF) == (0, GDN_HEADS, 2 * GDN_HEADS)

    meta = jnp.broadcast_to(meta_tokens.astype(x.dtype)[None], (bsz, N_META, d))
    h = jnp.concatenate([jnp.zeros((bsz, LEAD_PAD, d), x.dtype), meta, x], axis=1)

    rows = lambda v: v.reshape(depth, 1, -1).astype(F32)
    bf = lambda w: w.astype(BF16)
    ffn1 = (rows(ffn1_norm), bf(ffn1_w_gate), bf(ffn1_w_up), bf(ffn1_w_down))
    ffn2 = (rows(ffn2_norm), bf(ffn2_w_gate), bf(ffn2_w_up), bf(ffn2_w_down))
    proj = (rows(mix_norm), _pack_w_in(bf(w_in)))
    lru_p = (lru_conv_w, rows(lru_conv_b),
             jnp.concatenate([_block_diag(bf(lru_w_a)), _block_diag(bf(lru_w_i))], axis=-1),
             rows(jnp.concatenate([lru_b_a, lru_b_i], axis=-1)), rows(lru_lambda), rows(lru_norm))
    gdn_p = (gdn_conv_w, _small_rows(gdn_a_log, ALPHA_OFF), _small_rows(gdn_dt_bias, ALPHA_OFF), rows(gdn_norm))
    ssd_p = (ssd_conv_w, rows(ssd_conv_b), _small_rows(ssd_a_log, DT_OFF), _small_rows(ssd_dt_bias, DT_OFF),
             rows(jnp.repeat(ssd_d, SSD_HEAD_DIM, axis=1)), rows(ssd_norm))
    s5_w = _s5_prep(s5_a_re, s5_a_im, s5_log_dt, s5_b_re, s5_b_im, s5_c_re, s5_c_im)
    out_p = (rows(s5_d), bf(s5_w_glu), rows(s5_norm), bf(w_out))

    flat = lambda t: t.reshape(m, d)
    seq3 = lambda t: t.reshape(bsz, tp, d)
    for l in range(depth):
        h = seq3(_ffn(flat(h), l, *ffn1))
        a_x, a_gate, b_qkv, b_z, c_z, c_xbc, small, d_u = _inproj(h, l, *proj)
        y_a = _lru(a_x, a_gate, l, *lru_p)
        y_b = _gdn(b_qkv, b_z, small, l, *gdn_p)
        y_c = _ssd(c_z, c_xbc, small, l, *ssd_p)
        y_d = _s5_scan(d_u.reshape(S5_OCT, nsub * bsz * S5_SUB, LANES), l, *s5_w, bsz).reshape(d_u.shape)
        h = _outproj(h, y_a, y_b, y_c, y_d, d_u, l, *out_p)
        h = seq3(_ffn(flat(h), l, *ffn2))

    out = _final_norm(flat(h), final_norm.reshape(1, d))
    return out.reshape(bsz, tp, d)[:, LEAD_PAD + N_META:]
```

```python
import functools

import numpy as np
import jax
import jax.numpy as jnp
from jax import lax
from jax.experimental import pallas as pl
from jax.experimental.pallas import tpu as pltpu

F32 = jnp.float32
BF16 = jnp.bfloat16
HI = lax.Precision.HIGHEST

D_MODEL = 1024
N_META = 16
W_GRP = 512
CONV_K = 4
D_FF = 2816
EPS = 1e-6
CHUNK = 64
LEAD_PAD = CHUNK - N_META
LRU_C = 8.0
GDN_HEADS = 4
GDN_HEAD_DIM = 128
SSD_HEADS = 8
SSD_HEAD_DIM = 64
SSD_GROUPS = 2
SSD_STATE = 128
XBC_W = W_GRP + 2 * SSD_GROUPS * SSD_STATE
S5_GROUP_CH = 16
S5_GROUPS = 32
S5_STATE = 64

LANES = 128
SUBLANES = 8
SMALL_W = LANES
BETA_OFF, ALPHA_OFF, DT_OFF = 0, 4, 8
VMEM_LIMIT = 56 * 1024 * 1024

S5_SUB = SUBLANES
S5_OCT = W_GRP // LANES
S5_OCT_GROUPS = S5_GROUPS // S5_OCT
S5_VEC = S5_SUB * LANES
S5_NST = S5_OCT_GROUPS * S5_STATE

NN = (((1,), (0,)), ((), ()))
NT = (((1,), (1,)), ((), ()))


def _dot(a, b, dims=NN, prec=None):
    return lax.dot_general(a, b, dims, precision=prec, preferred_element_type=F32)


def _bdot(a, b, dims=NN):
    return lax.dot_general(a.astype(BF16), b.astype(BF16), dims, preferred_element_type=F32)


def _rms(x, g):
    return x * lax.rsqrt(jnp.mean(x * x, axis=-1, keepdims=True) + EPS) * g


def _silu(x):
    return x * jax.nn.sigmoid(x)


def _gelu_tanh(x):
    return 0.5 * x * (1.0 + jnp.tanh(np.sqrt(2.0 / np.pi).astype(np.float32) * (x + 0.044715 * (x * x * x))))


def _softplus(x):
    return jnp.maximum(x, 0.0) + jnp.log1p(jnp.exp(-jnp.abs(x)))


def _row_iota(shape):
    return lax.broadcasted_iota(jnp.int32, shape, 0)


def _col_iota(shape):
    return lax.broadcasted_iota(jnp.int32, shape, 1)


def _roll_groups(x, k):
    r, c = x.shape
    return pltpu.roll(x.reshape(r // SUBLANES, SUBLANES, c), k, axis=1).reshape(r, c)


def _shift_rows(x, prev8, k):
    xs = pltpu.roll(x, k, axis=0)
    ps = pltpu.roll(prev8, k, axis=0)
    top = jnp.where(_row_iota(ps.shape) < k, ps, xs[:SUBLANES])
    return jnp.concatenate([top, xs[SUBLANES:]], axis=0)


def _causal_conv4(x, prev8, w_ref):
    y = x * w_ref[CONV_K - 1:CONV_K, :]
    for k in range(1, CONV_K):
        y = y + _shift_rows(x, prev8, k) * w_ref[CONV_K - 1 - k:CONV_K - k, :]
    return y


def _compiler_params(sem):
    return pltpu.CompilerParams(dimension_semantics=sem, vmem_limit_bytes=VMEM_LIMIT)


def _layer_spec(arr, l):
    shape = arr.shape[1:]
    return pl.BlockSpec((None,) + shape, lambda *_: (l,) + (0,) * len(shape),
                        pipeline_mode=pl.Buffered(1))


def _valid_rows(row0, nrows, ncols):
    return _row_iota((nrows, ncols)) + row0 >= LEAD_PAD


def _seq_spec(tt, ncols):
    return pl.BlockSpec((None, tt, ncols), lambda b, t: (b, t, 0))


def _oct_spec(tt):
    return pl.BlockSpec((S5_OCT, tt // S5_SUB, None, S5_SUB, LANES), lambda b, t: (0, t, b, 0, 0))


FFN_TM = 1056
FFN_FC = 256


def _ffn_body(h_ref, g_ref, wg_ref, wu_ref, wd_ref, o_ref):
    h = h_ref[...]
    xn = _rms(h, g_ref[...]).astype(BF16)
    acc = jnp.zeros_like(h)
    for c in range(D_FF // FFN_FC):
        sl = slice(c * FFN_FC, (c + 1) * FFN_FC)
        gt = _dot(xn, wg_ref[:, sl])
        up = _dot(xn, wu_ref[:, sl])
        acc = acc + _dot((_silu(gt) * up).astype(BF16), wd_ref[sl, :])
    o_ref[...] = h + 0.5 * acc


def _ffn(h, l, g, wg, wu, wd):
    m = h.shape[0]
    tok = pl.BlockSpec((FFN_TM, D_MODEL), lambda i: (i, 0))
    return pl.pallas_call(
        _ffn_body,
        grid=(m // FFN_TM,),
        in_specs=[tok] + [_layer_spec(p, l) for p in (g, wg, wu, wd)],
        out_specs=tok,
        out_shape=jax.ShapeDtypeStruct(h.shape, F32),
        compiler_params=_compiler_params(("parallel",)),
        name="ffn",
    )(h, g, wg, wu, wd)


PROJ_TM = 704
PROJ_HBM_SLABS = (3 * W_GRP, W_GRP, W_GRP, XBC_W, SMALL_W)
PROJ_W = 2 * W_GRP + sum(PROJ_HBM_SLABS) + W_GRP
PROJ_PIECE = 512
LRU_ROWS = CHUNK


def _lru_tile(x, ug, prev8, carry_row, row0, cw_ref, cb_ref, wai_ref, bai_ref, lam_ref, ng_ref, between):
    n = x.shape[0]
    xc = _causal_conv4(x, prev8, cw_ref) + cb_ref[...]
    between()
    gates = _dot(xc.astype(BF16), wai_ref[...]) + bai_ref[...]
    between()
    r = jax.nn.sigmoid(gates[:, :W_GRP])
    ig = jax.nn.sigmoid(gates[:, W_GRP:])
    between()
    log_a = (-LRU_C * _softplus(-lam_ref[...])) * r
    a = jnp.exp(log_a)
    valid = _valid_rows(row0, n, W_GRP)
    bv = jnp.sqrt(jnp.tanh(-log_a) * (1.0 + a * a)) * (ig * xc)
    bv = jnp.where(valid, bv, 0.0)
    between()
    sub = _row_iota((n, W_GRP)) & (SUBLANES - 1)
    s = 1
    while s < SUBLANES:
        a_sh = jnp.where(sub < s, 1.0, _roll_groups(a, s))
        b_sh = jnp.where(sub < s, 0.0, _roll_groups(bv, s))
        bv = a * b_sh + bv
        a = a * a_sh
        s *= 2
        between()
    gate = _gelu_tanh(ug)
    between()
    groups = []
    for g in range(n // SUBLANES):
        rows = slice(g * SUBLANES, (g + 1) * SUBLANES)
        hg = bv[rows] + a[rows] * carry_row
        carry_row = hg[SUBLANES - 1:SUBLANES, :]
        groups.append(hg)
    between()
    y = _rms(gate * jnp.concatenate(groups, axis=0), ng_ref[...])
    return jnp.where(valid, y, 0.0), carry_row, x[n - SUBLANES:]


def _inproj_lru_body(h_ref, g_ref, w_ref, cw_ref, cb_ref, wai_ref, bai_ref, lam_ref, ng_ref,
                     qkv_ref, bz_ref, cz_ref, xbc_ref, sm_ref, u_ref, ya_ref,
                     ax_ref, ag_ref, prev_ref, hst_ref, *, tiles_per_seq):
    i = pl.program_id(0)

    @pl.when(i == 0)
    def _():
        ax_ref[...] = jnp.zeros_like(ax_ref)
        ag_ref[...] = jnp.zeros_like(ag_ref)
        prev_ref[...] = jnp.zeros_like(prev_ref)
        hst_ref[...] = jnp.zeros_like(hst_ref)

    xn = _rms(h_ref[...], g_ref[...]).astype(BF16)
    pieces = [(ax_ref.at[i % 2], 0, W_GRP, 0), (ag_ref.at[i % 2], 0, W_GRP, W_GRP)]
    off = 2 * W_GRP
    for o_ref in (qkv_ref, bz_ref, cz_ref, xbc_ref, sm_ref):
        n = o_ref.shape[-1]
        pieces += [(o_ref, c0, min(c0 + PROJ_PIECE, n), off + c0) for c0 in range(0, n, PROJ_PIECE)]
        off += n
    pieces.append((None, 0, W_GRP, off))
    todo = iter(pieces)

    def project(count=1):
        for _ in range(count):
            nxt = next(todo, None)
            if nxt is None:
                return
            o_ref, c0, c1, woff = nxt
            res = _dot(xn, w_ref[:, woff:woff + c1 - c0])
            if o_ref is None:
                for q in range(S5_OCT):
                    u_ref[q] = res[:, q * LANES:(q + 1) * LANES].reshape(PROJ_TM // S5_SUB, S5_SUB, LANES)
            else:
                o_ref[:, c0:c1] = res

    lru_tile = jnp.maximum(i - 1, 0) % tiles_per_seq
    first = lru_tile == 0
    x_prev = ax_ref[(i + 1) % 2]
    g_prev = ag_ref[(i + 1) % 2]
    carry = jnp.where(first, 0.0, hst_ref[0:1, :])
    prev8 = jnp.where(first, 0.0, prev_ref[...])
    for c in range(PROJ_TM // LRU_ROWS):
        rs = slice(c * LRU_ROWS, (c + 1) * LRU_ROWS)
        y, carry, prev8 = _lru_tile(x_prev[rs], g_prev[rs], prev8, carry, lru_tile * PROJ_TM + c * LRU_ROWS,
                                    cw_ref, cb_ref, wai_ref, bai_ref, lam_ref, ng_ref, lambda: None)
        ya_ref[rs, :] = y.astype(ya_ref.dtype)
        project()
    hst_ref[...] = jnp.broadcast_to(carry, hst_ref.shape)
    prev_ref[...] = prev8
    project(len(pieces))


def _inproj_lru(h, l, g, w, cw, cb, wai, bai, lam, ng):
    b, tp, _ = h.shape
    tps = tp // PROJ_TM
    nt = b * tps

    def cur(i):
        t = jnp.minimum(i, nt - 1)
        return t // tps, t % tps

    def lag(i):
        t = jnp.maximum(i - 1, 0)
        return t // tps, t % tps

    seq = lambda ncols, at: pl.BlockSpec((None, PROJ_TM, ncols), lambda i: at(i) + (0,))
    oct_spec = pl.BlockSpec((S5_OCT, PROJ_TM // S5_SUB, None, S5_SUB, LANES),
                            lambda i: (0, cur(i)[1], cur(i)[0], 0, 0))
    return pl.pallas_call(
        functools.partial(_inproj_lru_body, tiles_per_seq=tps),
        grid=(nt + 1,),
        in_specs=[seq(D_MODEL, cur)] + [_layer_spec(p, l) for p in (g, w, cw, cb, wai, bai, lam, ng)],
        out_specs=[seq(n, cur) for n in PROJ_HBM_SLABS] + [oct_spec, seq(W_GRP, lag)],
        out_shape=[jax.ShapeDtypeStruct((b, tp, n), F32) for n in PROJ_HBM_SLABS]
        + [jax.ShapeDtypeStruct((S5_OCT, tp // S5_SUB, b, S5_SUB, LANES), F32),
           jax.ShapeDtypeStruct((b, tp, W_GRP), BF16)],
        scratch_shapes=[pltpu.VMEM((2, PROJ_TM, W_GRP), F32), pltpu.VMEM((2, PROJ_TM, W_GRP), F32),
                        pltpu.VMEM((SUBLANES, W_GRP), F32), pltpu.VMEM((SUBLANES, W_GRP), F32)],
        compiler_params=_compiler_params(("arbitrary",)),
        name="inproj_lru",
    )(h, g, w, cw, cb, wai, bai, lam, ng)


GDN_NC = 11
SSD_NC = 3


def _tri_masks():
    r = _row_iota((CHUNK, CHUNK))
    c = _col_iota((CHUNK, CHUNK))
    return r >= c, r > c


def _chunk_cumsum(x):
    incl, _ = _tri_masks()
    tril = incl.astype(F32)
    triu = (_row_iota((CHUNK, CHUNK)) <= _col_iota((CHUNK, CHUNK))).astype(F32)
    return _dot(tril, x, prec=HI), _dot(x.T, triu, prec=HI)


def _gdn_body(qkv_ref, z_ref, sm_ref, cw_ref, alog_ref, dtb_ref, ng_ref, o_ref, prev_ref, s_ref, *, nc):
    t_idx = pl.program_id(1)
    tt = nc * CHUNK

    @pl.when(t_idx == 0)
    def _():
        prev_ref[...] = jnp.zeros_like(prev_ref)
        s_ref[...] = jnp.zeros_like(s_ref)

    incl, strict = _tri_masks()
    hd = GDN_HEAD_DIM
    x = qkv_ref[...]
    qkv = _silu(_causal_conv4(x, prev_ref[...], cw_ref))
    prev_ref[...] = x[tt - SUBLANES:]
    sm = sm_ref[...]
    beta_all = jax.nn.sigmoid(sm)
    g_all = -jnp.exp(alog_ref[...]) * _softplus(sm + dtb_ref[...])
    g_all = jnp.where(_valid_rows(t_idx * tt, tt, SMALL_W), g_all, 0.0)
    valid = _valid_rows(t_idx * tt, tt, hd)
    cums = [_chunk_cumsum(g_all[c * CHUNK:(c + 1) * CHUNK]) for c in range(nc)]

    heads = range(GDN_HEADS)
    pairs = [(h, c) for h in heads for c in range(nc)]
    rs = [slice(c * CHUNK, (c + 1) * CHUNK) for c in range(nc)]
    qs, ks, kbs, vbs = [], [], [], []
    for h in heads:
        q = qkv[:, h * hd:(h + 1) * hd]
        k = qkv[:, W_GRP + h * hd:W_GRP + (h + 1) * hd]
        v = qkv[:, 2 * W_GRP + h * hd:2 * W_GRP + (h + 1) * hd]
        qs.append(q * lax.rsqrt(jnp.sum(q * q, axis=-1, keepdims=True) + EPS) * (hd ** -0.5))
        k = k * lax.rsqrt(jnp.sum(k * k, axis=-1, keepdims=True) + EPS)
        beta = beta_all[:, BETA_OFF + h:BETA_OFF + h + 1]
        ks.append(k)
        kbs.append(k * beta)
        vbs.append(v * beta)
    gcol = {(h, c): cums[c][0][:, ALPHA_OFF + h:ALPHA_OFF + h + 1] for h, c in pairs}
    grow = {(h, c): cums[c][1][ALPHA_OFF + h:ALPHA_OFF + h + 1, :] for h, c in pairs}
    decay = {p: jnp.exp(jnp.where(incl, gcol[p] - grow[p], -jnp.inf)) for p in pairs}
    eg = {p: jnp.exp(gcol[p]) for p in pairs}
    g_last = {p: gcol[p][CHUNK - 1:CHUNK, :] for p in pairs}

    kk = {(h, c): _bdot(kbs[h][rs[c]], ks[h][rs[c]], NT) for h, c in pairs}
    qk = {(h, c): _bdot(qs[h][rs[c]], ks[h][rs[c]], NT) for h, c in pairs}
    nmat = {p: jnp.where(strict, kk[p] * decay[p], 0.0) for p in pairs}
    attn = {p: qk[p] * decay[p] for p in pairs}
    eye = (_row_iota((CHUNK, CHUNK)) == _col_iota((CHUNK, CHUNK))).astype(F32)
    tinv = {p: eye - nmat[p] for p in pairs}
    npow = {p: _bdot(nmat[p], nmat[p]) for p in pairs}
    span = 2
    while True:
        tinv = {p: tinv[p] + _bdot(tinv[p], npow[p]) for p in pairs}
        span *= 2
        if span >= CHUNK:
            break
        npow = {p: _bdot(npow[p], npow[p]) for p in pairs}
    u = {(h, c): _bdot(tinv[(h, c)], vbs[h][rs[c]]) for h, c in pairs}
    w = {(h, c): _bdot(tinv[(h, c)], kbs[h][rs[c]] * eg[(h, c)]) for h, c in pairs}
    q_dec = {(h, c): qs[h][rs[c]] * eg[(h, c)] for h, c in pairs}
    k_dec_t = {(h, c): (ks[h][rs[c]] * jnp.exp(g_last[(h, c)] - gcol[(h, c)])).T for h, c in pairs}

    s = [s_ref[h] for h in heads]
    outs = [[None] * nc for _ in heads]
    for c in range(nc):
        ws = [_bdot(w[(h, c)], s[h]) for h in heads]
        qd = [_bdot(q_dec[(h, c)], s[h]) for h in heads]
        v_new = [u[(h, c)] - ws[h] for h in heads]
        av = [_bdot(attn[(h, c)], v_new[h]) for h in heads]
        kv = [_bdot(k_dec_t[(h, c)], v_new[h]) for h in heads]
        for h in heads:
            outs[h][c] = qd[h] + av[h]
            s[h] = s[h] * jnp.exp(g_last[(h, c)]) + kv[h]
    for h in heads:
        s_ref[h] = s[h]
        o = jnp.concatenate(outs[h], axis=0)
        o = _rms(o, ng_ref[...]) * _silu(z_ref[:, h * hd:(h + 1) * hd])
        o = jnp.where(valid, o, 0.0)
        o_ref[:, h * hd:(h + 1) * hd] = o.astype(o_ref.dtype)


def _gdn(qkv, z, sm, l, cw, alog, dtb, ng):
    b, tp, _ = qkv.shape
    tt = GDN_NC * CHUNK
    return pl.pallas_call(
        functools.partial(_gdn_body, nc=GDN_NC),
        grid=(b, tp // tt),
        in_specs=[_seq_spec(tt, 3 * W_GRP), _seq_spec(tt, W_GRP), _seq_spec(tt, SMALL_W)]
        + [_layer_spec(p, l) for p in (cw, alog, dtb, ng)],
        out_specs=_seq_spec(tt, W_GRP),
        out_shape=jax.ShapeDtypeStruct((b, tp, W_GRP), BF16),
        scratch_shapes=[pltpu.VMEM((SUBLANES, 3 * W_GRP), F32),
                        pltpu.VMEM((GDN_HEADS, GDN_HEAD_DIM, GDN_HEAD_DIM), F32)],
        compiler_params=_compiler_params(("parallel", "arbitrary")),
        name="gdn",
    )(qkv, z, sm, cw, alog, dtb, ng)


SSD_PAIRS = SSD_HEADS // 2


def _ssd_body(z_ref, xbc_ref, sm_ref, cw_ref, cb_ref, alog_ref, dtb_ref, dsk_ref, ng_ref, o_ref,
              prev_ref, s_ref, *, nc):
    t_idx = pl.program_id(1)
    tt = nc * CHUNK

    @pl.when(t_idx == 0)
    def _():
        prev_ref[...] = jnp.zeros_like(prev_ref)
        s_ref[...] = jnp.zeros_like(s_ref)

    hp = SSD_HEAD_DIM
    gw = W_GRP // SSD_GROUPS
    ppg = SSD_PAIRS // SSD_GROUPS
    expand = (_row_iota((SMALL_W, W_GRP)) - DT_OFF
              == _col_iota((SMALL_W, W_GRP)) >> (hp.bit_length() - 1)).astype(F32)
    lane = _col_iota((CHUNK, LANES))
    lo_half = lane < hp
    incl2 = _row_iota((CHUNK, LANES)) >= (lane & (hp - 1))
    lo_rows = _row_iota((LANES, 1)) < hp

    x = xbc_ref[...]
    xbc = _silu(_causal_conv4(x, prev_ref[...], cw_ref) + cb_ref[...])
    prev_ref[...] = x[tt - SUBLANES:]
    xs = xbc[:, :W_GRP]
    dt_all = _softplus(sm_ref[...] + dtb_ref[...])
    dt_all = jnp.where(_valid_rows(t_idx * tt, tt, SMALL_W), dt_all, 0.0)
    adt_all = dt_all * -jnp.exp(alog_ref[...])
    x_in = xs * _dot(dt_all, expand, prec=HI)
    cums = [_chunk_cumsum(adt_all[c * CHUNK:(c + 1) * CHUNK]) for c in range(nc)]
    rs = [slice(c * CHUNK, (c + 1) * CHUNK) for c in range(nc)]
    pairs = [(j, c) for j in range(SSD_PAIRS) for c in range(nc)]
    grp = lambda j: j // ppg

    def head_cols(j, c):
        cum, cum_t = cums[c]
        h0 = DT_OFF + 2 * j
        acol = jnp.where(lo_half, cum[:, h0:h0 + 1], cum[:, h0 + 1:h0 + 2])
        arow = jnp.concatenate([cum_t[h0:h0 + 1, :], cum_t[h0 + 1:h0 + 2, :]], axis=1)
        return acol, arow

    b_all = [xbc[:, W_GRP + g * SSD_STATE:W_GRP + (g + 1) * SSD_STATE] for g in range(SSD_GROUPS)]
    c_all = [xbc[:, W_GRP + (SSD_GROUPS + g) * SSD_STATE:W_GRP + (SSD_GROUPS + g + 1) * SSD_STATE]
             for g in range(SSD_GROUPS)]
    cols = {p: head_cols(*p) for p in pairs}
    a_last = {p: cols[p][0][CHUNK - 1:CHUNK, :] for p in pairs}
    xp = {(j, c): x_in[rs[c], j * LANES:(j + 1) * LANES] for j, c in pairs}

    cb = {(g, c): _bdot(c_all[g][rs[c]], b_all[g][rs[c]], NT) for g in range(SSD_GROUPS) for c in range(nc)}
    cb2 = {k: jnp.concatenate([v, v], axis=1) for k, v in cb.items()}
    lm = {p: cb2[(grp(p[0]), p[1])] * jnp.exp(jnp.where(incl2, cols[p][0] - cols[p][1], -jnp.inf))
          for p in pairs}
    x_bd = {p: jnp.concatenate([jnp.where(lo_half, xp[p], 0.0), jnp.where(lo_half, 0.0, xp[p])], axis=0)
            for p in pairs}
    y_diag = {p: _bdot(lm[p], x_bd[p]) for p in pairs}
    st = {(j, c): _bdot((xp[(j, c)] * jnp.exp(a_last[(j, c)] - cols[(j, c)][0])).T, b_all[grp(j)][rs[c]])
          for j, c in pairs}
    s_in = {}
    for j in range(SSD_PAIRS):
        s = s_ref[j]
        for c in range(nc):
            s_in[(j, c)] = s
            e_last = jnp.exp(a_last[(j, c)])
            s = s * jnp.where(lo_rows, e_last[:, 0:1], e_last[:, hp:hp + 1]) + st[(j, c)]
        s_ref[j] = s
    y = {(j, c): y_diag[(j, c)] + _bdot(c_all[grp(j)][rs[c]], s_in[(j, c)], NT) * jnp.exp(cols[(j, c)][0])
         for j, c in pairs}
    y = jnp.concatenate([jnp.concatenate([y[(j, c)] for j in range(SSD_PAIRS)], axis=1)
                         for c in range(nc)], axis=0)
    y = (y + dsk_ref[...] * xs) * _silu(z_ref[...])
    y = jnp.concatenate(
        [_rms(y[:, g * gw:(g + 1) * gw], ng_ref[:, g * gw:(g + 1) * gw]) for g in range(SSD_GROUPS)],
        axis=1)
    y = jnp.where(_valid_rows(t_idx * tt, tt, W_GRP), y, 0.0)
    o_ref[...] = y.astype(o_ref.dtype)


def _ssd(z, xbc, sm, l, cw, cb, alog, dtb, dsk, ng):
    b, tp, _ = z.shape
    tt = SSD_NC * CHUNK
    return pl.pallas_call(
        functools.partial(_ssd_body, nc=SSD_NC),
        grid=(b, tp // tt),
        in_specs=[_seq_spec(tt, W_GRP), _seq_spec(tt, XBC_W), _seq_spec(tt, SMALL_W)]
        + [_layer_spec(p, l) for p in (cw, cb, alog, dtb, dsk, ng)],
        out_specs=_seq_spec(tt, W_GRP),
        out_shape=jax.ShapeDtypeStruct((b, tp, W_GRP), BF16),
        scratch_shapes=[pltpu.VMEM((SUBLANES, XBC_W), F32),
                        pltpu.VMEM((SSD_PAIRS, 2 * SSD_HEAD_DIM, SSD_STATE), F32)],
        compiler_params=_compiler_params(("parallel", "arbitrary")),
        name="ssd",
    )(z, xbc, sm, cw, cb, alog, dtb, dsk, ng)


def _s5_discretise(a_re, a_im, log_dt):
    lam_re = jnp.minimum(a_re, -1e-4)
    lam_im = a_im
    dt = jnp.exp(log_dt)
    mag = jnp.exp(dt * lam_re)
    ab_re = mag * jnp.cos(dt * lam_im)
    ab_im = mag * jnp.sin(dt * lam_im)
    den = lam_re * lam_re + lam_im * lam_im
    f_re = ((ab_re - 1.0) * lam_re + ab_im * lam_im) / den
    f_im = (ab_im * lam_re - (ab_re - 1.0) * lam_im) / den
    return ab_re, ab_im, f_re, f_im


def _s5_prep_body(ar_ref, ai_ref, ld_ref, arc_ref, aic_ref, ldc_ref, btr_ref, bti_ref, ctr_ref, cti_ref,
                  w1_ref, wc_ref, apow_ref):
    ab_re, ab_im, f_re, f_im = _s5_discretise(ar_ref[...], ai_ref[...], ld_ref[...])
    abc_re, abc_im, _, _ = _s5_discretise(arc_ref[...], aic_ref[...], ldc_ref[...])
    bb_re = f_re * btr_ref[...] - f_im * bti_ref[...]
    bb_im = f_re * bti_ref[...] + f_im * btr_ref[...]
    ct_re = ctr_ref[...]
    ct_im = cti_ref[...]

    pows = [(jnp.ones_like(ab_re), jnp.zeros_like(ab_re))]
    for _ in range(S5_SUB):
        p_re, p_im = pows[-1]
        pows.append((p_re * ab_re - p_im * ab_im, p_re * ab_im + p_im * ab_re))
    x_re = jnp.concatenate([bb_re * pows[S5_SUB - 1 - s][0] - bb_im * pows[S5_SUB - 1 - s][1]
                            for s in range(S5_SUB)], axis=0)
    x_im = jnp.concatenate([bb_re * pows[S5_SUB - 1 - s][1] + bb_im * pows[S5_SUB - 1 - s][0]
                            for s in range(S5_SUB)], axis=0)
    taps = _dot(x_re, ct_re, prec=HI) - _dot(x_im, ct_im, prec=HI)
    cols = []
    for l in range(S5_SUB):
        up = (S5_SUB - 1 - l) * LANES
        cols.append(taps[up:] if up == 0 else
                    jnp.concatenate([taps[up:], jnp.zeros((up, LANES), F32)], axis=0))
    w1_ref[...] = jnp.concatenate(cols + [x_re, x_im], axis=1).astype(w1_ref.dtype)
    apow_ref[...] = jnp.concatenate(pows[S5_SUB], axis=1)

    q_re, q_im = abc_re, abc_im
    wc_re, wc_im = [], []
    for l in range(S5_SUB):
        wc_re.append(ct_re * q_re - ct_im * q_im)
        wc_im.append(-(ct_re * q_im + ct_im * q_re))
        q_re, q_im = q_re * abc_re - q_im * abc_im, q_re * abc_im + q_im * abc_re
    wc_ref[...] = jnp.concatenate([jnp.concatenate(wc_re, axis=1), jnp.concatenate(wc_im, axis=1)],
                                  axis=0).astype(wc_ref.dtype)


def _block_diag(w):
    s, i, j = w.shape[-3:]
    eye = jnp.eye(s, dtype=w.dtype)
    return (w[..., :, :, None, :] * eye[:, None, :, None]).reshape(w.shape[:-3] + (s * i, s * j))


def _s5_prep(a_re, a_im, log_dt, b_re, b_im, c_re, c_im):
    depth = a_re.shape[0]
    row = lambda v: v.reshape(depth, S5_OCT, 1, S5_NST)
    col = lambda v: v.reshape(depth, S5_OCT, S5_NST, 1)
    ld = jnp.repeat(log_dt, S5_STATE, axis=1)
    octs = lambda w: _block_diag(jnp.swapaxes(w, 2, 3).reshape(
        (depth, S5_OCT, S5_OCT_GROUPS) + (w.shape[3], w.shape[2])))

    def ospec(shape):
        return pl.BlockSpec((None, None) + shape, lambda d, q: (d, q) + (0,) * len(shape))

    outs = [((S5_VEC, S5_VEC + 2 * S5_NST), BF16), ((2 * S5_NST, S5_VEC), BF16), ((1, 2 * S5_NST), F32)]
    return pl.pallas_call(
        _s5_prep_body,
        grid=(depth, S5_OCT),
        in_specs=[ospec((1, S5_NST))] * 3 + [ospec((S5_NST, 1))] * 3
        + [ospec((LANES, S5_NST))] * 2 + [ospec((S5_NST, LANES))] * 2,
        out_specs=[ospec(s) for s, _ in outs],
        out_shape=[jax.ShapeDtypeStruct((depth, S5_OCT) + s, d) for s, d in outs],
        compiler_params=_compiler_params(("parallel", "parallel")),
        name="s5_prep",
    )(row(a_re), row(a_im), row(ld), col(a_re), col(a_im), col(ld),
      octs(b_re), octs(b_im), octs(c_re), octs(c_im))


S5_RT = 704


def _s5_body(u_ref, w1_ref, wc_ref, apow_ref, y_ref, loc_ref, sin_ref, st_ref):
    @pl.when(pl.program_id(1) == 0)
    def _():
        st_ref[...] = jnp.zeros_like(st_ref)

    nb = st_ref.shape[0]
    u = jnp.concatenate([u_ref[pl.ds(s, S5_RT, stride=S5_SUB), :] for s in range(S5_SUB)], axis=1)
    z = _dot(u.astype(BF16), w1_ref[...])
    loc_ref[...] = z[:, S5_VEC:]
    a_re = apow_ref[:, :S5_NST]
    a_im = apow_ref[:, S5_NST:]

    def step(c, carry):
        s_re, s_im = carry
        r0 = pl.multiple_of(c * nb, nb)
        sin_ref[pl.ds(r0, nb), :S5_NST] = s_re
        sin_ref[pl.ds(r0, nb), S5_NST:] = s_im
        n_re = a_re * s_re - a_im * s_im + loc_ref[pl.ds(r0, nb), :S5_NST]
        n_im = a_re * s_im + a_im * s_re + loc_ref[pl.ds(r0, nb), S5_NST:]
        return n_re, n_im

    s_re, s_im = lax.fori_loop(0, S5_RT // nb, step, (st_ref[:, :S5_NST], st_ref[:, S5_NST:]))
    st_ref[:, :S5_NST] = s_re
    st_ref[:, S5_NST:] = s_im
    y = z[:, :S5_VEC] + _dot(sin_ref[...].astype(BF16), wc_ref[...])
    for l in range(S5_SUB):
        y_ref[pl.ds(l, S5_RT, stride=S5_SUB), :] = y[:, l * LANES:(l + 1) * LANES]


def _s5_scan(u, l, w1, wc, apow, nb):
    noct, rows, _ = u.shape

    def wspec(arr):
        shape = arr.shape[2:]
        return pl.BlockSpec((None, None) + shape, lambda q, r: (l, q) + (0,) * len(shape))

    tile = pl.BlockSpec((None, S5_RT * S5_SUB, LANES), lambda q, r: (q, r, 0))
    return pl.pallas_call(
        _s5_body,
        grid=(noct, rows // (S5_RT * S5_SUB)),
        in_specs=[tile, wspec(w1), wspec(wc), wspec(apow)],
        out_specs=tile,
        out_shape=jax.ShapeDtypeStruct(u.shape, F32),
        scratch_shapes=[pltpu.VMEM((S5_RT, 2 * S5_NST), F32), pltpu.VMEM((S5_RT, 2 * S5_NST), F32),
                        pltpu.VMEM((nb, 2 * S5_NST), F32)],
        compiler_params=_compiler_params(("parallel", "arbitrary")),
        name="s5_scan",
    )(u, w1, wc, apow)


OUT_TM = 704


def _outproj_body(h_ref, ya_ref, yb_ref, yc_ref, yd_ref, ud_ref, dsk_ref, wglu_ref, ng_ref, wo_ref, o_ref):
    tok = lambda ref: jnp.concatenate([ref[q].reshape(OUT_TM, LANES) for q in range(S5_OCT)], axis=1)
    yd = tok(yd_ref) + dsk_ref[...] * tok(ud_ref)
    yd = _gelu_tanh(yd)
    yd = yd * jax.nn.sigmoid(_dot(yd.astype(BF16), wglu_ref[...]))
    yd = _rms(yd, ng_ref[...]).astype(BF16)
    acc = _dot(ya_ref[...], wo_ref[0:W_GRP, :])
    acc = acc + _dot(yb_ref[...], wo_ref[W_GRP:2 * W_GRP, :])
    acc = acc + _dot(yc_ref[...], wo_ref[2 * W_GRP:3 * W_GRP, :])
    acc = acc + _dot(yd, wo_ref[3 * W_GRP:4 * W_GRP, :])
    o_ref[...] = h_ref[...] + acc


def _outproj(h, ya, yb, yc, yd, ud, l, dsk, wglu, ng, wo):
    b, tp, _ = h.shape
    tok = _seq_spec(OUT_TM, D_MODEL)
    grp = _seq_spec(OUT_TM, W_GRP)
    return pl.pallas_call(
        _outproj_body,
        grid=(b, tp // OUT_TM),
        in_specs=[tok, grp, grp, grp, _oct_spec(OUT_TM), _oct_spec(OUT_TM)]
        + [_layer_spec(p, l) for p in (dsk, wglu, ng, wo)],
        out_specs=tok,
        out_shape=jax.ShapeDtypeStruct(h.shape, F32),
        compiler_params=_compiler_params(("parallel", "parallel")),
        name="outproj",
    )(h, ya, yb, yc, yd, ud, dsk, wglu, ng, wo)


def _final_norm_body(h_ref, g_ref, o_ref):
    o_ref[...] = _rms(h_ref[...], g_ref[...])


def _final_norm(h, g):
    m = h.shape[0]
    tok = pl.BlockSpec((FFN_TM, D_MODEL), lambda i: (i, 0))
    return pl.pallas_call(
        _final_norm_body,
        grid=(m // FFN_TM,),
        in_specs=[tok, pl.BlockSpec((1, D_MODEL), lambda i: (0, 0))],
        out_specs=tok,
        out_shape=jax.ShapeDtypeStruct(h.shape, F32),
        compiler_params=_compiler_params(("parallel",)),
        name="final_norm",
    )(h, g)


def _small_rows(vals, off):
    depth, n = vals.shape
    return jnp.zeros((depth, 1, SMALL_W), F32).at[:, 0, off:off + n].set(vals)


def _pack_w_in(w_in):
    sizes = (W_GRP, W_GRP, 3 * W_GRP, W_GRP, GDN_HEADS, GDN_HEADS, W_GRP, XBC_W, SSD_HEADS, W_GRP)
    offs = np.concatenate([[0], np.cumsum(sizes)])
    a_x, a_gate, b_qkv, b_z, b_beta, b_alpha, c_z, c_xbc, c_dt, d_u = (
        w_in[..., offs[i]:offs[i + 1]] for i in range(len(sizes)))
    pad = lambda n: jnp.zeros(w_in.shape[:-1] + (n,), w_in.dtype)
    small = jnp.concatenate([b_beta, b_alpha, c_dt, pad(SMALL_W - DT_OFF - SSD_HEADS)], axis=-1)
    return jnp.concatenate([a_x, a_gate, b_qkv, b_z, c_z, c_xbc, small, d_u], axis=-1)


def kernel(x, meta_tokens, ffn1_norm, ffn1_w_gate, ffn1_w_up, ffn1_w_down, mix_norm, w_in, w_out,
           lru_conv_w, lru_conv_b, lru_w_a, lru_b_a, lru_w_i, lru_b_i, lru_lambda, lru_norm,
           gdn_conv_w, gdn_a_log, gdn_dt_bias, gdn_norm,
           ssd_conv_w, ssd_conv_b, ssd_a_log, ssd_dt_bias, ssd_d, ssd_norm,
           s5_a_re, s5_a_im, s5_log_dt, s5_b_re, s5_b_im, s5_c_re, s5_c_im, s5_d, s5_w_glu, s5_norm,
           ffn2_norm, ffn2_w_gate, ffn2_w_up, ffn2_w_down, final_norm):
    bsz, seq, d = x.shape
    depth = w_in.shape[0]
    tp = LEAD_PAD + N_META + seq
    m = bsz * tp
    nsub = tp // S5_SUB
    assert d == D_MODEL and bsz == SUBLANES and m % FFN_TM == 0
    assert all(tp % t == 0 for t in (PROJ_TM, GDN_NC * CHUNK, SSD_NC * CHUNK, OUT_TM))
    assert (nsub * bsz) % S5_RT == 0
    assert (BETA_OFF, ALPHA_OFF, DT_OFF) == (0, GDN_HEADS, 2 * GDN_HEADS)

    meta = jnp.broadcast_to(meta_tokens.astype(x.dtype)[None], (bsz, N_META, d))
    h = jnp.concatenate([jnp.zeros((bsz, LEAD_PAD, d), x.dtype), meta, x], axis=1)

    rows = lambda v: v.reshape(depth, 1, -1).astype(F32)
    bf = lambda w: w.astype(BF16)
    ffn1 = (rows(ffn1_norm), bf(ffn1_w_gate), bf(ffn1_w_up), bf(ffn1_w_down))
    ffn2 = (rows(ffn2_norm), bf(ffn2_w_gate), bf(ffn2_w_up), bf(ffn2_w_down))
    proj = (rows(mix_norm), _pack_w_in(bf(w_in)))
    lru_p = (lru_conv_w, rows(lru_conv_b),
             jnp.concatenate([_block_diag(bf(lru_w_a)), _block_diag(bf(lru_w_i))], axis=-1),
             rows(jnp.concatenate([lru_b_a, lru_b_i], axis=-1)), rows(lru_lambda), rows(lru_norm))
    gdn_p = (gdn_conv_w, _small_rows(gdn_a_log, ALPHA_OFF), _small_rows(gdn_dt_bias, ALPHA_OFF), rows(gdn_norm))
    ssd_p = (ssd_conv_w, rows(ssd_conv_b), _small_rows(ssd_a_log, DT_OFF), _small_rows(ssd_dt_bias, DT_OFF),
             rows(jnp.repeat(ssd_d, SSD_HEAD_DIM, axis=1)), rows(ssd_norm))
    s5_w = _s5_prep(s5_a_re, s5_a_im, s5_log_dt, s5_b_re, s5_b_im, s5_c_re, s5_c_im)
    out_p = (rows(s5_d), bf(s5_w_glu), rows(s5_norm), bf(w_out))

    flat = lambda t: t.reshape(m, d)
    seq3 = lambda t: t.reshape(bsz, tp, d)
    for l in range(depth):
        h = seq3(_ffn(flat(h), l, *ffn1))
        b_qkv, b_z, c_z, c_xbc, small, d_u, y_a = _inproj_lru(h, l, *proj, *lru_p)
        y_b = _gdn(b_qkv, b_z, small, l, *gdn_p)
        y_c = _ssd(c_z, c_xbc, small, l, *ssd_p)
        y_d = _s5_scan(d_u.reshape(S5_OCT, nsub * bsz * S5_SUB, LANES), l, *s5_w, bsz).reshape(d_u.shape)
        h = _outproj(h, y_a, y_b, y_c, y_d, d_u, l, *out_p)
        h = seq3(_ffn(flat(h), l, *ffn2))

    out = _final_norm(flat(h), final_norm.reshape(1, d))
    return out.reshape(bsz, tp, d)[:, LEAD_PAD + N_META:]
```

```python
import functools

import numpy as np
import jax
import jax.numpy as jnp
from jax import lax
from jax.experimental import pallas as pl
from jax.experimental.pallas import tpu as pltpu

F32 = jnp.float32
BF16 = jnp.bfloat16
HI = lax.Precision.HIGHEST

D_MODEL = 1024
N_META = 16
W_GRP = 512
CONV_K = 4
D_FF = 2816
EPS = 1e-6
CHUNK = 64
LEAD_PAD = CHUNK - N_META
LRU_C = 8.0
GDN_HEADS = 4
GDN_HEAD_DIM = 128
SSD_HEADS = 8
SSD_HEAD_DIM = 64
SSD_GROUPS = 2
SSD_STATE = 128
XBC_W = W_GRP + 2 * SSD_GROUPS * SSD_STATE
S5_GROUP_CH = 16
S5_GROUPS = 32
S5_STATE = 64

LANES = 128
SUBLANES = 8
SMALL_W = LANES
BETA_OFF, ALPHA_OFF, DT_OFF = 0, 4, 8
VMEM_LIMIT = 56 * 1024 * 1024

S5_SUB = SUBLANES
S5_OCT = W_GRP // LANES
S5_OCT_GROUPS = S5_GROUPS // S5_OCT
S5_VEC = S5_SUB * LANES
S5_NST = S5_OCT_GROUPS * S5_STATE

NN = (((1,), (0,)), ((), ()))
NT = (((1,), (1,)), ((), ()))


def _dot(a, b, dims=NN, prec=None):
    return lax.dot_general(a, b, dims, precision=prec, preferred_element_type=F32)


def _bdot(a, b, dims=NN):
    return lax.dot_general(a.astype(BF16), b.astype(BF16), dims, preferred_element_type=F32)


def _rms(x, g):
    return x * lax.rsqrt(jnp.mean(x * x, axis=-1, keepdims=True) + EPS) * g


def _silu(x):
    return x * jax.nn.sigmoid(x)


def _gelu_tanh(x):
    return 0.5 * x * (1.0 + jnp.tanh(np.sqrt(2.0 / np.pi).astype(np.float32) * (x + 0.044715 * (x * x * x))))


def _softplus(x):
    return jnp.maximum(x, 0.0) + jnp.log1p(jnp.exp(-jnp.abs(x)))


def _row_iota(shape):
    return lax.broadcasted_iota(jnp.int32, shape, 0)


def _col_iota(shape):
    return lax.broadcasted_iota(jnp.int32, shape, 1)


def _roll_groups(x, k):
    r, c = x.shape
    return pltpu.roll(x.reshape(r // SUBLANES, SUBLANES, c), k, axis=1).reshape(r, c)


def _shift_rows(x, prev8, k):
    xs = pltpu.roll(x, k, axis=0)
    ps = pltpu.roll(prev8, k, axis=0)
    top = jnp.where(_row_iota(ps.shape) < k, ps, xs[:SUBLANES])
    return jnp.concatenate([top, xs[SUBLANES:]], axis=0)


def _causal_conv4(x, prev8, w_ref):
    y = x * w_ref[CONV_K - 1:CONV_K, :]
    for k in range(1, CONV_K):
        y = y + _shift_rows(x, prev8, k) * w_ref[CONV_K - 1 - k:CONV_K - k, :]
    return y


def _compiler_params(sem):
    return pltpu.CompilerParams(dimension_semantics=sem, vmem_limit_bytes=VMEM_LIMIT)


def _layer_spec(arr, l):
    shape = arr.shape[1:]
    return pl.BlockSpec((None,) + shape, lambda *_: (l,) + (0,) * len(shape),
                        pipeline_mode=pl.Buffered(1))


def _valid_rows(row0, nrows, ncols):
    return _row_iota((nrows, ncols)) + row0 >= LEAD_PAD


def _seq_spec(tt, ncols):
    return pl.BlockSpec((None, tt, ncols), lambda b, t: (b, t, 0))


FFN_TM = 1056
FFN_FC = 256


def _ffn_body(h_ref, g_ref, wg_ref, wu_ref, wd_ref, o_ref):
    h = h_ref[...]
    xn = _rms(h, g_ref[...]).astype(BF16)
    acc = jnp.zeros_like(h)
    for c in range(D_FF // FFN_FC):
        sl = slice(c * FFN_FC, (c + 1) * FFN_FC)
        gt = _dot(xn, wg_ref[:, sl])
        up = _dot(xn, wu_ref[:, sl])
        acc = acc + _dot((_silu(gt) * up).astype(BF16), wd_ref[sl, :])
    o_ref[...] = h + 0.5 * acc


def _ffn(h, l, g, wg, wu, wd):
    m = h.shape[0]
    tok = pl.BlockSpec((FFN_TM, D_MODEL), lambda i: (i, 0))
    return pl.pallas_call(
        _ffn_body,
        grid=(m // FFN_TM,),
        in_specs=[tok] + [_layer_spec(p, l) for p in (g, wg, wu, wd)],
        out_specs=tok,
        out_shape=jax.ShapeDtypeStruct(h.shape, F32),
        compiler_params=_compiler_params(("parallel",)),
        name="ffn",
    )(h, g, wg, wu, wd)


PROJ_TM = 704
PROJ_HBM_SLABS = (3 * W_GRP, W_GRP, W_GRP, XBC_W, SMALL_W)
PROJ_W = 2 * W_GRP + sum(PROJ_HBM_SLABS) + W_GRP
PROJ_PIECE = 512
LRU_ROWS = 176


def _lru_tile(x, ug, prev8, carry_row, row0, cw_ref, cb_ref, wai_ref, bai_ref, lam_ref, ng_ref):
    n = x.shape[0]
    xc = _causal_conv4(x, prev8, cw_ref) + cb_ref[...]
    gates = _dot(xc.astype(BF16), wai_ref[...]) + bai_ref[...]
    r = jax.nn.sigmoid(gates[:, :W_GRP])
    ig = jax.nn.sigmoid(gates[:, W_GRP:])
    log_a = (-LRU_C * _softplus(-lam_ref[...])) * r
    a = jnp.exp(log_a)
    valid = _valid_rows(row0, n, W_GRP)
    bv = jnp.sqrt(jnp.tanh(-log_a) * (1.0 + a * a)) * (ig * xc)
    bv = jnp.where(valid, bv, 0.0)
    sub = _row_iota((n, W_GRP)) & (SUBLANES - 1)
    s = 1
    while s < SUBLANES:
        a_sh = jnp.where(sub < s, 1.0, _roll_groups(a, s))
        b_sh = jnp.where(sub < s, 0.0, _roll_groups(bv, s))
        bv = a * b_sh + bv
        a = a * a_sh
        s *= 2
    groups = []
    for g in range(n // SUBLANES):
        rows = slice(g * SUBLANES, (g + 1) * SUBLANES)
        hg = bv[rows] + a[rows] * carry_row
        carry_row = hg[SUBLANES - 1:SUBLANES, :]
        groups.append(hg)
    y = _rms(_gelu_tanh(ug) * jnp.concatenate(groups, axis=0), ng_ref[...])
    return jnp.where(valid, y, 0.0), carry_row, x[n - SUBLANES:]


def _inproj_lru_body(h_ref, g_ref, w_ref, cw_ref, cb_ref, wai_ref, bai_ref, lam_ref, ng_ref,
                     qkv_ref, bz_ref, cz_ref, xbc_ref, sm_ref, u_ref, ya_ref,
                     ax_ref, ag_ref, prev_ref, hst_ref, *, tiles_per_seq):
    i = pl.program_id(0)

    @pl.when(i == 0)
    def _():
        ax_ref[...] = jnp.zeros_like(ax_ref)
        ag_ref[...] = jnp.zeros_like(ag_ref)
        prev_ref[...] = jnp.zeros_like(prev_ref)
        hst_ref[...] = jnp.zeros_like(hst_ref)

    xn = _rms(h_ref[...], g_ref[...]).astype(BF16)
    pieces = [(ax_ref.at[i % 2], 0, W_GRP, 0), (ag_ref.at[i % 2], 0, W_GRP, W_GRP)]
    off = 2 * W_GRP
    for o_ref in (qkv_ref, bz_ref, cz_ref, xbc_ref, sm_ref):
        n = o_ref.shape[-1]
        pieces += [(o_ref, c0, min(c0 + PROJ_PIECE, n), off + c0) for c0 in range(0, n, PROJ_PIECE)]
        off += n
    pieces.append((None, 0, W_GRP, off))
    todo = iter(pieces)

    def project(count=1):
        for _ in range(count):
            nxt = next(todo, None)
            if nxt is None:
                return
            o_ref, c0, c1, woff = nxt
            res = _dot(xn, w_ref[:, woff:woff + c1 - c0])
            if o_ref is None:
                for q in range(S5_OCT):
                    u_ref[q] = res[:, q * LANES:(q + 1) * LANES].reshape(PROJ_TM // S5_SUB, S5_SUB, LANES)
            else:
                o_ref[:, c0:c1] = res

    lru_tile = jnp.maximum(i - 1, 0) % tiles_per_seq
    first = lru_tile == 0
    x_prev = ax_ref[(i + 1) % 2]
    g_prev = ag_ref[(i + 1) % 2]
    carry = jnp.where(first, 0.0, hst_ref[0:1, :])
    prev8 = jnp.where(first, 0.0, prev_ref[...])
    nsub = PROJ_TM // LRU_ROWS
    per_sub = -(-len(pieces) // nsub)
    for c in range(nsub):
        rs = slice(c * LRU_ROWS, (c + 1) * LRU_ROWS)
        y, carry, prev8 = _lru_tile(x_prev[rs], g_prev[rs], prev8, carry, lru_tile * PROJ_TM + c * LRU_ROWS,
                                    cw_ref, cb_ref, wai_ref, bai_ref, lam_ref, ng_ref)
        ya_ref[rs, :] = y.astype(ya_ref.dtype)
        project(per_sub)
    hst_ref[...] = jnp.broadcast_to(carry, hst_ref.shape)
    prev_ref[...] = prev8
    project(len(pieces))


def _inproj_lru(h, l, g, w, cw, cb, wai, bai, lam, ng):
    b, tp, _ = h.shape
    tps = tp // PROJ_TM
    nt = b * tps

    def cur(i):
        t = jnp.minimum(i, nt - 1)
        return t // tps, t % tps

    def lag(i):
        t = jnp.maximum(i - 1, 0)
        return t // tps, t % tps

    seq = lambda ncols, at: pl.BlockSpec((None, PROJ_TM, ncols), lambda i: at(i) + (0,))
    oct_spec = pl.BlockSpec((S5_OCT, PROJ_TM // S5_SUB, None, S5_SUB, LANES),
                            lambda i: (0, cur(i)[1], cur(i)[0], 0, 0))
    return pl.pallas_call(
        functools.partial(_inproj_lru_body, tiles_per_seq=tps),
        grid=(nt + 1,),
        in_specs=[seq(D_MODEL, cur)] + [_layer_spec(p, l) for p in (g, w, cw, cb, wai, bai, lam, ng)],
        out_specs=[seq(n, cur) for n in PROJ_HBM_SLABS] + [oct_spec, seq(W_GRP, lag)],
        out_shape=[jax.ShapeDtypeStruct((b, tp, n), F32) for n in PROJ_HBM_SLABS]
        + [jax.ShapeDtypeStruct((S5_OCT, tp // S5_SUB, b, S5_SUB, LANES), F32),
           jax.ShapeDtypeStruct((b, tp, W_GRP), BF16)],
        scratch_shapes=[pltpu.VMEM((2, PROJ_TM, W_GRP), F32), pltpu.VMEM((2, PROJ_TM, W_GRP), F32),
                        pltpu.VMEM((SUBLANES, W_GRP), F32), pltpu.VMEM((SUBLANES, W_GRP), F32)],
        compiler_params=_compiler_params(("arbitrary",)),
        name="inproj_lru",
    )(h, g, w, cw, cb, wai, bai, lam, ng)


GDN_NC = 11
SSD_NC = 3


def _tri_masks():
    r = _row_iota((CHUNK, CHUNK))
    c = _col_iota((CHUNK, CHUNK))
    return r >= c, r > c


def _chunk_cumsum(x):
    incl, _ = _tri_masks()
    tril = incl.astype(F32)
    triu = (_row_iota((CHUNK, CHUNK)) <= _col_iota((CHUNK, CHUNK))).astype(F32)
    return _dot(tril, x, prec=HI), _dot(x.T, triu, prec=HI)


def _gdn_body(qkv_ref, z_ref, sm_ref, cw_ref, alog_ref, dtb_ref, ng_ref, o_ref, prev_ref, s_ref, *, nc):
    t_idx = pl.program_id(1)
    tt = nc * CHUNK

    @pl.when(t_idx == 0)
    def _():
        prev_ref[...] = jnp.zeros_like(prev_ref)
        s_ref[...] = jnp.zeros_like(s_ref)

    incl, strict = _tri_masks()
    hd = GDN_HEAD_DIM
    x = qkv_ref[...]
    qkv = _silu(_causal_conv4(x, prev_ref[...], cw_ref))
    prev_ref[...] = x[tt - SUBLANES:]
    sm = sm_ref[...]
    beta_all = jax.nn.sigmoid(sm)
    g_all = -jnp.exp(alog_ref[...]) * _softplus(sm + dtb_ref[...])
    g_all = jnp.where(_valid_rows(t_idx * tt, tt, SMALL_W), g_all, 0.0)
    valid = _valid_rows(t_idx * tt, tt, hd)
    cums = [_chunk_cumsum(g_all[c * CHUNK:(c + 1) * CHUNK]) for c in range(nc)]

    heads = range(GDN_HEADS)
    pairs = [(h, c) for h in heads for c in range(nc)]
    rs = [slice(c * CHUNK, (c + 1) * CHUNK) for c in range(nc)]
    qs, ks, kbs, vbs = [], [], [], []
    for h in heads:
        q = qkv[:, h * hd:(h + 1) * hd]
        k = qkv[:, W_GRP + h * hd:W_GRP + (h + 1) * hd]
        v = qkv[:, 2 * W_GRP + h * hd:2 * W_GRP + (h + 1) * hd]
        qs.append(q * lax.rsqrt(jnp.sum(q * q, axis=-1, keepdims=True) + EPS) * (hd ** -0.5))
        k = k * lax.rsqrt(jnp.sum(k * k, axis=-1, keepdims=True) + EPS)
        beta = beta_all[:, BETA_OFF + h:BETA_OFF + h + 1]
        ks.append(k)
        kbs.append(k * beta)
        vbs.append(v * beta)
    gcol = {(h, c): cums[c][0][:, ALPHA_OFF + h:ALPHA_OFF + h + 1] for h, c in pairs}
    grow = {(h, c): cums[c][1][ALPHA_OFF + h:ALPHA_OFF + h + 1, :] for h, c in pairs}
    decay = {p: jnp.exp(jnp.where(incl, gcol[p] - grow[p], -jnp.inf)) for p in pairs}
    eg = {p: jnp.exp(gcol[p]) for p in pairs}
    g_last = {p: gcol[p][CHUNK - 1:CHUNK, :] for p in pairs}

    kk = {(h, c): _bdot(kbs[h][rs[c]], ks[h][rs[c]], NT) for h, c in pairs}
    qk = {(h, c): _bdot(qs[h][rs[c]], ks[h][rs[c]], NT) for h, c in pairs}
    nmat = {p: jnp.where(strict, kk[p] * decay[p], 0.0) for p in pairs}
    attn = {p: qk[p] * decay[p] for p in pairs}
    eye = (_row_iota((CHUNK, CHUNK)) == _col_iota((CHUNK, CHUNK))).astype(F32)
    tinv = {p: eye - nmat[p] for p in pairs}
    npow = {p: _bdot(nmat[p], nmat[p]) for p in pairs}
    span = 2
    while True:
        tinv = {p: tinv[p] + _bdot(tinv[p], npow[p]) for p in pairs}
        span *= 2
        if span >= CHUNK:
            break
        npow = {p: _bdot(npow[p], npow[p]) for p in pairs}
    u = {(h, c): _bdot(tinv[(h, c)], vbs[h][rs[c]]) for h, c in pairs}
    w = {(h, c): _bdot(tinv[(h, c)], kbs[h][rs[c]] * eg[(h, c)]) for h, c in pairs}
    q_dec = {(h, c): qs[h][rs[c]] * eg[(h, c)] for h, c in pairs}
    k_dec_t = {(h, c): (ks[h][rs[c]] * jnp.exp(g_last[(h, c)] - gcol[(h, c)])).T for h, c in pairs}

    s = [s_ref[h] for h in heads]
    outs = [[None] * nc for _ in heads]
    for c in range(nc):
        ws = [_bdot(w[(h, c)], s[h]) for h in heads]
        qd = [_bdot(q_dec[(h, c)], s[h]) for h in heads]
        v_new = [u[(h, c)] - ws[h] for h in heads]
        av = [_bdot(attn[(h, c)], v_new[h]) for h in heads]
        kv = [_bdot(k_dec_t[(h, c)], v_new[h]) for h in heads]
        for h in heads:
            outs[h][c] = qd[h] + av[h]
            s[h] = s[h] * jnp.exp(g_last[(h, c)]) + kv[h]
    for h in heads:
        s_ref[h] = s[h]
        o = jnp.concatenate(outs[h], axis=0)
        o = _rms(o, ng_ref[...]) * _silu(z_ref[:, h * hd:(h + 1) * hd])
        o = jnp.where(valid, o, 0.0)
        o_ref[:, h * hd:(h + 1) * hd] = o.astype(o_ref.dtype)


def _gdn(qkv, z, sm, l, cw, alog, dtb, ng):
    b, tp, _ = qkv.shape
    tt = GDN_NC * CHUNK
    return pl.pallas_call(
        functools.partial(_gdn_body, nc=GDN_NC),
        grid=(b, tp // tt),
        in_specs=[_seq_spec(tt, 3 * W_GRP), _seq_spec(tt, W_GRP), _seq_spec(tt, SMALL_W)]
        + [_layer_spec(p, l) for p in (cw, alog, dtb, ng)],
        out_specs=_seq_spec(tt, W_GRP),
        out_shape=jax.ShapeDtypeStruct((b, tp, W_GRP), BF16),
        scratch_shapes=[pltpu.VMEM((SUBLANES, 3 * W_GRP), F32),
                        pltpu.VMEM((GDN_HEADS, GDN_HEAD_DIM, GDN_HEAD_DIM), F32)],
        compiler_params=_compiler_params(("parallel", "arbitrary")),
        name="gdn",
    )(qkv, z, sm, cw, alog, dtb, ng)


SSD_PAIRS = SSD_HEADS // 2


def _ssd_block(xbc, z, sm, states, row0, alog_ref, dtb_ref, dsk_ref, ng_ref):
    tt = xbc.shape[0]
    nc = tt // CHUNK
    hp = SSD_HEAD_DIM
    gw = W_GRP // SSD_GROUPS
    ppg = SSD_PAIRS // SSD_GROUPS
    expand = (_row_iota((SMALL_W, W_GRP)) - DT_OFF
              == _col_iota((SMALL_W, W_GRP)) >> (hp.bit_length() - 1)).astype(F32)
    lane = _col_iota((CHUNK, LANES))
    lo_half = lane < hp
    incl2 = _row_iota((CHUNK, LANES)) >= (lane & (hp - 1))
    lo_rows = _row_iota((LANES, 1)) < hp

    xs = xbc[:, :W_GRP]
    dt_all = _softplus(sm + dtb_ref[...])
    dt_all = jnp.where(_valid_rows(row0, tt, SMALL_W), dt_all, 0.0)
    adt_all = dt_all * -jnp.exp(alog_ref[...])
    x_in = xs * _dot(dt_all, expand, prec=HI)
    cums = [_chunk_cumsum(adt_all[c * CHUNK:(c + 1) * CHUNK]) for c in range(nc)]
    rs = [slice(c * CHUNK, (c + 1) * CHUNK) for c in range(nc)]
    pairs = [(j, c) for j in range(SSD_PAIRS) for c in range(nc)]
    grp = lambda j: j // ppg

    def head_cols(j, c):
        cum, cum_t = cums[c]
        h0 = DT_OFF + 2 * j
        acol = jnp.where(lo_half, cum[:, h0:h0 + 1], cum[:, h0 + 1:h0 + 2])
        arow = jnp.concatenate([cum_t[h0:h0 + 1, :], cum_t[h0 + 1:h0 + 2, :]], axis=1)
        return acol, arow

    b_all = [xbc[:, W_GRP + g * SSD_STATE:W_GRP + (g + 1) * SSD_STATE] for g in range(SSD_GROUPS)]
    c_all = [xbc[:, W_GRP + (SSD_GROUPS + g) * SSD_STATE:W_GRP + (SSD_GROUPS + g + 1) * SSD_STATE]
             for g in range(SSD_GROUPS)]
    cols = {p: head_cols(*p) for p in pairs}
    a_last = {p: cols[p][0][CHUNK - 1:CHUNK, :] for p in pairs}
    xp = {(j, c): x_in[rs[c], j * LANES:(j + 1) * LANES] for j, c in pairs}

    cb = {(g, c): _bdot(c_all[g][rs[c]], b_all[g][rs[c]], NT) for g in range(SSD_GROUPS) for c in range(nc)}
    cb2 = {k: jnp.concatenate([v, v], axis=1) for k, v in cb.items()}
    lm = {p: cb2[(grp(p[0]), p[1])] * jnp.exp(jnp.where(incl2, cols[p][0] - cols[p][1], -jnp.inf))
          for p in pairs}
    x_bd = {p: jnp.concatenate([jnp.where(lo_half, xp[p], 0.0), jnp.where(lo_half, 0.0, xp[p])], axis=0)
            for p in pairs}
    y_diag = {p: _bdot(lm[p], x_bd[p]) for p in pairs}
    st = {(j, c): _bdot((xp[(j, c)] * jnp.exp(a_last[(j, c)] - cols[(j, c)][0])).T, b_all[grp(j)][rs[c]])
          for j, c in pairs}
    s_in = {}
    new_states = []
    for j in range(SSD_PAIRS):
        s = states[j]
        for c in range(nc):
            s_in[(j, c)] = s
            e_last = jnp.exp(a_last[(j, c)])
            s = s * jnp.where(lo_rows, e_last[:, 0:1], e_last[:, hp:hp + 1]) + st[(j, c)]
        new_states.append(s)
    y = {(j, c): y_diag[(j, c)] + _bdot(c_all[grp(j)][rs[c]], s_in[(j, c)], NT) * jnp.exp(cols[(j, c)][0])
         for j, c in pairs}
    y = jnp.concatenate([jnp.concatenate([y[(j, c)] for j in range(SSD_PAIRS)], axis=1)
                         for c in range(nc)], axis=0)
    y = (y + dsk_ref[...] * xs) * _silu(z)
    y = jnp.concatenate(
        [_rms(y[:, g * gw:(g + 1) * gw], ng_ref[:, g * gw:(g + 1) * gw]) for g in range(SSD_GROUPS)],
        axis=1)
    return jnp.where(_valid_rows(row0, tt, W_GRP), y, 0.0), new_states


OUT_TM = 704


def _ssd_outproj_body(z_ref, xbc_ref, sm_ref, cw_ref, cb_ref, alog_ref, dtb_ref, dsk_ref, ng_ref,
                      h_ref, ya_ref, yb_ref, yd_ref, ud_ref, s5d_ref, wglu_ref, s5ng_ref, wo_ref,
                      o_ref, yc_ref, prev_ref, s_ref, *, tiles_per_seq):
    i = pl.program_id(0)
    last = pl.num_programs(0) - 2

    @pl.when(i == 0)
    def _():
        yc_ref[...] = jnp.zeros_like(yc_ref)
        prev_ref[...] = jnp.zeros_like(prev_ref)
        s_ref[...] = jnp.zeros_like(s_ref)

    yc_prev = yc_ref.at[(i + 1) % 2]
    yc_cur = yc_ref.at[i % 2]
    tok = lambda ref: jnp.concatenate([ref[q].reshape(OUT_TM, LANES) for q in range(S5_OCT)], axis=1)
    acc = {}

    def s5_epilogue():
        yd = tok(yd_ref) + s5d_ref[...] * tok(ud_ref)
        yd = _gelu_tanh(yd)
        yd = yd * jax.nn.sigmoid(_dot(yd.astype(BF16), wglu_ref[...]))
        acc["yd"] = _rms(yd, s5ng_ref[...]).astype(BF16)
        acc["o"] = h_ref[...] + _dot(ya_ref[...], wo_ref[0:W_GRP, :])

    def proj_b():
        acc["o"] = acc["o"] + _dot(yb_ref[...], wo_ref[W_GRP:2 * W_GRP, :])

    def proj_c():
        acc["o"] = acc["o"] + _dot(yc_prev[...], wo_ref[2 * W_GRP:3 * W_GRP, :])

    def proj_d():
        o_ref[...] = acc["o"] + _dot(acc["yd"], wo_ref[3 * W_GRP:4 * W_GRP, :])

    between = iter((s5_epilogue, proj_b, proj_c, proj_d))

    tile = jnp.minimum(i, last) % tiles_per_seq
    first = tile == 0
    prev8 = jnp.where(first, 0.0, prev_ref[...])
    states = [jnp.where(first, 0.0, s_ref[j]) for j in range(SSD_PAIRS)]
    nchunks = OUT_TM // CHUNK
    blocks = [slice(c0 * CHUNK, min(c0 + SSD_NC, nchunks) * CHUNK) for c0 in range(0, nchunks, SSD_NC)]
    xbc = []
    for rs in blocks:
        x = xbc_ref[rs, :]
        xbc.append(_silu(_causal_conv4(x, prev8, cw_ref) + cb_ref[...]))
        prev8 = x[x.shape[0] - SUBLANES:]
        next(between)()
    for rs, act in zip(blocks, xbc):
        y, states = _ssd_block(act, z_ref[rs, :], sm_ref[rs, :], states, tile * OUT_TM + rs.start,
                               alog_ref, dtb_ref, dsk_ref, ng_ref)
        yc_cur[rs, :] = y.astype(yc_cur.dtype)
    prev_ref[...] = prev8
    for j in range(SSD_PAIRS):
        s_ref[j] = states[j]


def _ssd_outproj(z, xbc, sm, h, ya, yb, yd, ud, l, ssd_p, out_p):
    b, tp, _ = h.shape
    tps = tp // OUT_TM
    nt = b * tps
    assert -(-(OUT_TM // CHUNK) // SSD_NC) == 4

    def cur(i):
        t = jnp.minimum(i, nt - 1)
        return t // tps, t % tps

    def lag(i):
        t = jnp.maximum(i - 1, 0)
        return t // tps, t % tps

    seq = lambda ncols, at: pl.BlockSpec((None, OUT_TM, ncols), lambda i: at(i) + (0,))
    oct_spec = pl.BlockSpec((S5_OCT, OUT_TM // S5_SUB, None, S5_SUB, LANES),
                            lambda i: (0, lag(i)[1], lag(i)[0], 0, 0))
    return pl.pallas_call(
        functools.partial(_ssd_outproj_body, tiles_per_seq=tps),
        grid=(nt + 1,),
        in_specs=[seq(W_GRP, cur), seq(XBC_W, cur), seq(SMALL_W, cur)] + [_layer_spec(p, l) for p in ssd_p]
        + [seq(D_MODEL, lag), seq(W_GRP, lag), seq(W_GRP, lag), oct_spec, oct_spec]
        + [_layer_spec(p, l) for p in out_p],
        out_specs=seq(D_MODEL, lag),
        out_shape=jax.ShapeDtypeStruct(h.shape, F32),
        scratch_shapes=[pltpu.VMEM((2, OUT_TM, W_GRP), BF16), pltpu.VMEM((SUBLANES, XBC_W), F32),
                        pltpu.VMEM((SSD_PAIRS, 2 * SSD_HEAD_DIM, SSD_STATE), F32)],
        compiler_params=_compiler_params(("arbitrary",)),
        name="ssd_outproj",
    )(z, xbc, sm, *ssd_p, h, ya, yb, yd, ud, *out_p)


def _s5_discretise(a_re, a_im, log_dt):
    lam_re = jnp.minimum(a_re, -1e-4)
    lam_im = a_im
    dt = jnp.exp(log_dt)
    mag = jnp.exp(dt * lam_re)
    ab_re = mag * jnp.cos(dt * lam_im)
    ab_im = mag * jnp.sin(dt * lam_im)
    den = lam_re * lam_re + lam_im * lam_im
    f_re = ((ab_re - 1.0) * lam_re + ab_im * lam_im) / den
    f_im = (ab_im * lam_re - (ab_re - 1.0) * lam_im) / den
    return ab_re, ab_im, f_re, f_im


def _s5_prep_body(ar_ref, ai_ref, ld_ref, arc_ref, aic_ref, ldc_ref, btr_ref, bti_ref, ctr_ref, cti_ref,
                  w1_ref, wc_ref, apow_ref):
    ab_re, ab_im, f_re, f_im = _s5_discretise(ar_ref[...], ai_ref[...], ld_ref[...])
    abc_re, abc_im, _, _ = _s5_discretise(arc_ref[...], aic_ref[...], ldc_ref[...])
    bb_re = f_re * btr_ref[...] - f_im * bti_ref[...]
    bb_im = f_re * bti_ref[...] + f_im * btr_ref[...]
    ct_re = ctr_ref[...]
    ct_im = cti_ref[...]

    pows = [(jnp.ones_like(ab_re), jnp.zeros_like(ab_re))]
    for _ in range(S5_SUB):
        p_re, p_im = pows[-1]
        pows.append((p_re * ab_re - p_im * ab_im, p_re * ab_im + p_im * ab_re))
    x_re = jnp.concatenate([bb_re * pows[S5_SUB - 1 - s][0] - bb_im * pows[S5_SUB - 1 - s][1]
                            for s in range(S5_SUB)], axis=0)
    x_im = jnp.concatenate([bb_re * pows[S5_SUB - 1 - s][1] + bb_im * pows[S5_SUB - 1 - s][0]
                            for s in range(S5_SUB)], axis=0)
    taps = _dot(x_re, ct_re, prec=HI) - _dot(x_im, ct_im, prec=HI)
    cols = []
    for l in range(S5_SUB):
        up = (S5_SUB - 1 - l) * LANES
        cols.append(taps[up:] if up == 0 else
                    jnp.concatenate([taps[up:], jnp.zeros((up, LANES), F32)], axis=0))
    w1_ref[...] = jnp.concatenate(cols + [x_re, x_im], axis=1).astype(w1_ref.dtype)
    apow_ref[...] = jnp.concatenate(pows[S5_SUB], axis=1)

    q_re, q_im = abc_re, abc_im
    wc_re, wc_im = [], []
    for l in range(S5_SUB):
        wc_re.append(ct_re * q_re - ct_im * q_im)
        wc_im.append(-(ct_re * q_im + ct_im * q_re))
        q_re, q_im = q_re * abc_re - q_im * abc_im, q_re * abc_im + q_im * abc_re
    wc_ref[...] = jnp.concatenate([jnp.concatenate(wc_re, axis=1), jnp.concatenate(wc_im, axis=1)],
                                  axis=0).astype(wc_ref.dtype)


def _block_diag(w):
    s, i, j = w.shape[-3:]
    eye = jnp.eye(s, dtype=w.dtype)
    return (w[..., :, :, None, :] * eye[:, None, :, None]).reshape(w.shape[:-3] + (s * i, s * j))


def _s5_prep(a_re, a_im, log_dt, b_re, b_im, c_re, c_im):
    depth = a_re.shape[0]
    row = lambda v: v.reshape(depth, S5_OCT, 1, S5_NST)
    col = lambda v: v.reshape(depth, S5_OCT, S5_NST, 1)
    ld = jnp.repeat(log_dt, S5_STATE, axis=1)
    octs = lambda w: _block_diag(jnp.swapaxes(w, 2, 3).reshape(
        (depth, S5_OCT, S5_OCT_GROUPS) + (w.shape[3], w.shape[2])))

    def ospec(shape):
        return pl.BlockSpec((None, None) + shape, lambda d, q: (d, q) + (0,) * len(shape))

    outs = [((S5_VEC, S5_VEC + 2 * S5_NST), BF16), ((2 * S5_NST, S5_VEC), BF16), ((1, 2 * S5_NST), F32)]
    return pl.pallas_call(
        _s5_prep_body,
        grid=(depth, S5_OCT),
        in_specs=[ospec((1, S5_NST))] * 3 + [ospec((S5_NST, 1))] * 3
        + [ospec((LANES, S5_NST))] * 2 + [ospec((S5_NST, LANES))] * 2,
        out_specs=[ospec(s) for s, _ in outs],
        out_shape=[jax.ShapeDtypeStruct((depth, S5_OCT) + s, d) for s, d in outs],
        compiler_params=_compiler_params(("parallel", "parallel")),
        name="s5_prep",
    )(row(a_re), row(a_im), row(ld), col(a_re), col(a_im), col(ld),
      octs(b_re), octs(b_im), octs(c_re), octs(c_im))


S5_RT = 704


def _s5_front(u_ref, w1_ref, loc_ref):
    u = jnp.concatenate([u_ref[pl.ds(s, S5_RT, stride=S5_SUB), :] for s in range(S5_SUB)], axis=1)
    z = _dot(u.astype(BF16), w1_ref[...])
    loc_ref[...] = z[:, S5_VEC:]
    return z[:, :S5_VEC]


def _s5_back(y_intra, wc_ref, apow_ref, y_ref, loc_ref, sin_ref, st_ref):
    nb = st_ref.shape[0]
    a_re = apow_ref[:, :S5_NST]
    a_im = apow_ref[:, S5_NST:]

    def step(c, carry):
        s_re, s_im = carry
        r0 = pl.multiple_of(c * nb, nb)
        sin_ref[pl.ds(r0, nb), :S5_NST] = s_re
        sin_ref[pl.ds(r0, nb), S5_NST:] = s_im
        n_re = a_re * s_re - a_im * s_im + loc_ref[pl.ds(r0, nb), :S5_NST]
        n_im = a_re * s_im + a_im * s_re + loc_ref[pl.ds(r0, nb), S5_NST:]
        return n_re, n_im

    s_re, s_im = lax.fori_loop(0, S5_RT // nb, step, (st_ref[:, :S5_NST], st_ref[:, S5_NST:]))
    st_ref[:, :S5_NST] = s_re
    st_ref[:, S5_NST:] = s_im
    y = y_intra + _dot(sin_ref[...].astype(BF16), wc_ref[...])
    for l in range(S5_SUB):
        y_ref[pl.ds(l, S5_RT, stride=S5_SUB), :] = y[:, l * LANES:(l + 1) * LANES]


def _s5_body(u_ref, w1_ref, wc_ref, apow_ref, y_ref, loc_ref, sin_ref, st_ref):
    @pl.when(pl.program_id(1) == 0)
    def _():
        st_ref[...] = jnp.zeros_like(st_ref)

    y_intra = _s5_front(u_ref, w1_ref, loc_ref)
    _s5_back(y_intra, wc_ref, apow_ref, y_ref, loc_ref, sin_ref, st_ref)


def _s5_scan(u, l, w1, wc, apow, nb):
    noct, rows, _ = u.shape

    def wspec(arr):
        shape = arr.shape[2:]
        return pl.BlockSpec((None, None) + shape, lambda q, r: (l, q) + (0,) * len(shape))

    tile = pl.BlockSpec((None, S5_RT * S5_SUB, LANES), lambda q, r: (q, r, 0))
    return pl.pallas_call(
        _s5_body,
        grid=(noct, rows // (S5_RT * S5_SUB)),
        in_specs=[tile, wspec(w1), wspec(wc), wspec(apow)],
        out_specs=tile,
        out_shape=jax.ShapeDtypeStruct(u.shape, F32),
        scratch_shapes=[pltpu.VMEM((S5_RT, 2 * S5_NST), F32), pltpu.VMEM((S5_RT, 2 * S5_NST), F32),
                        pltpu.VMEM((nb, 2 * S5_NST), F32)],
        compiler_params=_compiler_params(("parallel", "arbitrary")),
        name="s5_scan",
    )(u, w1, wc, apow)


def _final_norm_body(h_ref, g_ref, o_ref):
    skip = h_ref.shape[0] - o_ref.shape[0]
    o_ref[...] = _rms(h_ref[skip:, :], g_ref[...])


def _final_norm(h, g, seq):
    b, tp, d = h.shape
    return pl.pallas_call(
        _final_norm_body,
        grid=(b,),
        in_specs=[pl.BlockSpec((None, tp, d), lambda i: (i, 0, 0)), pl.BlockSpec((1, d), lambda i: (0, 0))],
        out_specs=pl.BlockSpec((None, seq, d), lambda i: (i, 0, 0)),
        out_shape=jax.ShapeDtypeStruct((b, seq, d), F32),
        compiler_params=_compiler_params(("parallel",)),
        name="final_norm",
    )(h, g)


def _small_rows(vals, off):
    depth, n = vals.shape
    return jnp.zeros((depth, 1, SMALL_W), F32).at[:, 0, off:off + n].set(vals)


def _pack_w_in(w_in):
    sizes = (W_GRP, W_GRP, 3 * W_GRP, W_GRP, GDN_HEADS, GDN_HEADS, W_GRP, XBC_W, SSD_HEADS, W_GRP)
    offs = np.concatenate([[0], np.cumsum(sizes)])
    a_x, a_gate, b_qkv, b_z, b_beta, b_alpha, c_z, c_xbc, c_dt, d_u = (
        w_in[..., offs[i]:offs[i + 1]] for i in range(len(sizes)))
    pad = lambda n: jnp.zeros(w_in.shape[:-1] + (n,), w_in.dtype)
    small = jnp.concatenate([b_beta, b_alpha, c_dt, pad(SMALL_W - DT_OFF - SSD_HEADS)], axis=-1)
    return jnp.concatenate([a_x, a_gate, b_qkv, b_z, c_z, c_xbc, small, d_u], axis=-1)


def kernel(x, meta_tokens, ffn1_norm, ffn1_w_gate, ffn1_w_up, ffn1_w_down, mix_norm, w_in, w_out,
           lru_conv_w, lru_conv_b, lru_w_a, lru_b_a, lru_w_i, lru_b_i, lru_lambda, lru_norm,
           gdn_conv_w, gdn_a_log, gdn_dt_bias, gdn_norm,
           ssd_conv_w, ssd_conv_b, ssd_a_log, ssd_dt_bias, ssd_d, ssd_norm,
           s5_a_re, s5_a_im, s5_log_dt, s5_b_re, s5_b_im, s5_c_re, s5_c_im, s5_d, s5_w_glu, s5_norm,
           ffn2_norm, ffn2_w_gate, ffn2_w_up, ffn2_w_down, final_norm):
    bsz, seq, d = x.shape
    depth = w_in.shape[0]
    tp = LEAD_PAD + N_META + seq
    m = bsz * tp
    nsub = tp // S5_SUB
    assert d == D_MODEL and bsz == SUBLANES and m % FFN_TM == 0
    assert all(tp % t == 0 for t in (PROJ_TM, GDN_NC * CHUNK, OUT_TM))
    assert (nsub * bsz) % S5_RT == 0
    assert (BETA_OFF, ALPHA_OFF, DT_OFF) == (0, GDN_HEADS, 2 * GDN_HEADS)

    meta = jnp.broadcast_to(meta_tokens.astype(x.dtype)[None], (bsz, N_META, d))
    h = jnp.concatenate([jnp.zeros((bsz, LEAD_PAD, d), x.dtype), meta, x], axis=1)

    rows = lambda v: v.reshape(depth, 1, -1).astype(F32)
    bf = lambda w: w.astype(BF16)
    ffn1 = (rows(ffn1_norm), bf(ffn1_w_gate), bf(ffn1_w_up), bf(ffn1_w_down))
    ffn2 = (rows(ffn2_norm), bf(ffn2_w_gate), bf(ffn2_w_up), bf(ffn2_w_down))
    proj = (rows(mix_norm), _pack_w_in(bf(w_in)))
    lru_p = (lru_conv_w, rows(lru_conv_b),
             jnp.concatenate([_block_diag(bf(lru_w_a)), _block_diag(bf(lru_w_i))], axis=-1),
             rows(jnp.concatenate([lru_b_a, lru_b_i], axis=-1)), rows(lru_lambda), rows(lru_norm))
    gdn_p = (gdn_conv_w, _small_rows(gdn_a_log, ALPHA_OFF), _small_rows(gdn_dt_bias, ALPHA_OFF), rows(gdn_norm))
    ssd_p = (ssd_conv_w, rows(ssd_conv_b), _small_rows(ssd_a_log, DT_OFF), _small_rows(ssd_dt_bias, DT_OFF),
             rows(jnp.repeat(ssd_d, SSD_HEAD_DIM, axis=1)), rows(ssd_norm))
    s5_w = _s5_prep(s5_a_re, s5_a_im, s5_log_dt, s5_b_re, s5_b_im, s5_c_re, s5_c_im)
    out_p = (rows(s5_d), bf(s5_w_glu), rows(s5_norm), bf(w_out))

    flat = lambda t: t.reshape(m, d)
    seq3 = lambda t: t.reshape(bsz, tp, d)
    for l in range(depth):
        h = seq3(_ffn(flat(h), l, *ffn1))
        b_qkv, b_z, c_z, c_xbc, small, d_u, y_a = _inproj_lru(h, l, *proj, *lru_p)
        y_b = _gdn(b_qkv, b_z, small, l, *gdn_p)
        y_d = _s5_scan(d_u.reshape(S5_OCT, nsub * bsz * S5_SUB, LANES), l, *s5_w, bsz).reshape(d_u.shape)
        h = _ssd_outproj(c_z, c_xbc, small, h, y_a, y_b, y_d, d_u, l, ssd_p, out_p)
        h = seq3(_ffn(flat(h), l, *ffn2))

    return _final_norm(h, final_norm.reshape(1, d), seq)
```

```python
import functools

import numpy as np
import jax
import jax.numpy as jnp
from jax import lax
from jax.experimental import pallas as pl
from jax.experimental.pallas import tpu as pltpu

F32 = jnp.float32
BF16 = jnp.bfloat16
HI = lax.Precision.HIGHEST

D_MODEL = 1024
N_META = 16
W_GRP = 512
CONV_K = 4
D_FF = 2816
EPS = 1e-6
CHUNK = 64
LEAD_PAD = CHUNK - N_META
LRU_C = 8.0
GDN_HEADS = 4
GDN_HEAD_DIM = 128
SSD_HEADS = 8
SSD_HEAD_DIM = 64
SSD_GROUPS = 2
SSD_STATE = 128
XBC_W = W_GRP + 2 * SSD_GROUPS * SSD_STATE
S5_GROUP_CH = 16
S5_GROUPS = 32
S5_STATE = 64

LANES = 128
SUBLANES = 8
SMALL_W = LANES
BETA_OFF, ALPHA_OFF, DT_OFF = 0, 4, 8
VMEM_LIMIT = 56 * 1024 * 1024

S5_SUB = SUBLANES
S5_OCT = W_GRP // LANES
S5_OCT_GROUPS = S5_GROUPS // S5_OCT
S5_VEC = S5_SUB * LANES
S5_NST = S5_OCT_GROUPS * S5_STATE

NN = (((1,), (0,)), ((), ()))
NT = (((1,), (1,)), ((), ()))


def _dot(a, b, dims=NN, prec=None):
    return lax.dot_general(a, b, dims, precision=prec, preferred_element_type=F32)


def _bdot(a, b, dims=NN):
    return lax.dot_general(a.astype(BF16), b.astype(BF16), dims, preferred_element_type=F32)


def _rms(x, g):
    return x * lax.rsqrt(jnp.mean(x * x, axis=-1, keepdims=True) + EPS) * g


def _silu(x):
    return x * jax.nn.sigmoid(x)


def _gelu_tanh(x):
    return 0.5 * x * (1.0 + jnp.tanh(np.sqrt(2.0 / np.pi).astype(np.float32) * (x + 0.044715 * (x * x * x))))


def _softplus(x):
    return jnp.maximum(x, 0.0) + jnp.log1p(jnp.exp(-jnp.abs(x)))


def _row_iota(shape):
    return lax.broadcasted_iota(jnp.int32, shape, 0)


def _col_iota(shape):
    return lax.broadcasted_iota(jnp.int32, shape, 1)


def _roll_groups(x, k):
    r, c = x.shape
    return pltpu.roll(x.reshape(r // SUBLANES, SUBLANES, c), k, axis=1).reshape(r, c)


def _shift_rows(x, prev8, k):
    xs = pltpu.roll(x, k, axis=0)
    ps = pltpu.roll(prev8, k, axis=0)
    top = jnp.where(_row_iota(ps.shape) < k, ps, xs[:SUBLANES])
    return jnp.concatenate([top, xs[SUBLANES:]], axis=0)


def _causal_conv4(x, prev8, w_ref):
    y = x * w_ref[CONV_K - 1:CONV_K, :]
    for k in range(1, CONV_K):
        y = y + _shift_rows(x, prev8, k) * w_ref[CONV_K - 1 - k:CONV_K - k, :]
    return y


def _compiler_params(sem):
    return pltpu.CompilerParams(dimension_semantics=sem, vmem_limit_bytes=VMEM_LIMIT)


def _layer_spec(arr, l):
    shape = arr.shape[1:]
    return pl.BlockSpec((None,) + shape, lambda *_: (l,) + (0,) * len(shape),
                        pipeline_mode=pl.Buffered(1))


def _valid_rows(row0, nrows, ncols):
    return _row_iota((nrows, ncols)) + row0 >= LEAD_PAD


def _seq_spec(tt, ncols):
    return pl.BlockSpec((None, tt, ncols), lambda b, t: (b, t, 0))


FFN_TM = 1056
FFN_FC = 256


def _ffn_body(h_ref, g_ref, wg_ref, wu_ref, wd_ref, o_ref):
    h = h_ref[...]
    xn = _rms(h, g_ref[...]).astype(BF16)
    acc = jnp.zeros_like(h)
    for c in range(D_FF // FFN_FC):
        sl = slice(c * FFN_FC, (c + 1) * FFN_FC)
        gt = _dot(xn, wg_ref[:, sl])
        up = _dot(xn, wu_ref[:, sl])
        acc = acc + _dot((_silu(gt) * up).astype(BF16), wd_ref[sl, :])
    o_ref[...] = h + 0.5 * acc


def _ffn(h, l, g, wg, wu, wd):
    m = h.shape[0]
    tok = pl.BlockSpec((FFN_TM, D_MODEL), lambda i: (i, 0))
    return pl.pallas_call(
        _ffn_body,
        grid=(m // FFN_TM,),
        in_specs=[tok] + [_layer_spec(p, l) for p in (g, wg, wu, wd)],
        out_specs=tok,
        out_shape=jax.ShapeDtypeStruct(h.shape, F32),
        compiler_params=_compiler_params(("parallel",)),
        name="ffn",
    )(h, g, wg, wu, wd)


PROJ_TM = 704
PROJ_HBM_SLABS = (3 * W_GRP, W_GRP, W_GRP, XBC_W, SMALL_W)
PROJ_W = 2 * W_GRP + sum(PROJ_HBM_SLABS) + W_GRP
PROJ_PIECE = 512
LRU_ROWS = 176


def _lru_tile(x, ug, prev8, carry_row, row0, cw_ref, cb_ref, wai_ref, bai_ref, lam_ref, ng_ref):
    n = x.shape[0]
    xc = _causal_conv4(x, prev8, cw_ref) + cb_ref[...]
    gates = _dot(xc.astype(BF16), wai_ref[...]) + bai_ref[...]
    r = jax.nn.sigmoid(gates[:, :W_GRP])
    ig = jax.nn.sigmoid(gates[:, W_GRP:])
    log_a = (-LRU_C * _softplus(-lam_ref[...])) * r
    a = jnp.exp(log_a)
    valid = _valid_rows(row0, n, W_GRP)
    bv = jnp.sqrt(jnp.tanh(-log_a) * (1.0 + a * a)) * (ig * xc)
    bv = jnp.where(valid, bv, 0.0)
    sub = _row_iota((n, W_GRP)) & (SUBLANES - 1)
    s = 1
    while s < SUBLANES:
        a_sh = jnp.where(sub < s, 1.0, _roll_groups(a, s))
        b_sh = jnp.where(sub < s, 0.0, _roll_groups(bv, s))
        bv = a * b_sh + bv
        a = a * a_sh
        s *= 2
    groups = []
    for g in range(n // SUBLANES):
        rows = slice(g * SUBLANES, (g + 1) * SUBLANES)
        hg = bv[rows] + a[rows] * carry_row
        carry_row = hg[SUBLANES - 1:SUBLANES, :]
        groups.append(hg)
    y = _rms(_gelu_tanh(ug) * jnp.concatenate(groups, axis=0), ng_ref[...])
    return jnp.where(valid, y, 0.0), carry_row, x[n - SUBLANES:]


def _inproj_lru_body(h_ref, g_ref, w_ref, cw_ref, cb_ref, wai_ref, bai_ref, lam_ref, ng_ref,
                     qkv_ref, bz_ref, cz_ref, xbc_ref, sm_ref, u_ref, ya_ref,
                     ax_ref, ag_ref, prev_ref, hst_ref, *, tiles_per_seq):
    i = pl.program_id(0)

    @pl.when(i == 0)
    def _():
        ax_ref[...] = jnp.zeros_like(ax_ref)
        ag_ref[...] = jnp.zeros_like(ag_ref)
        prev_ref[...] = jnp.zeros_like(prev_ref)
        hst_ref[...] = jnp.zeros_like(hst_ref)

    xn = _rms(h_ref[...], g_ref[...]).astype(BF16)
    pieces = [(ax_ref.at[i % 2], 0, W_GRP, 0), (ag_ref.at[i % 2], 0, W_GRP, W_GRP)]
    off = 2 * W_GRP
    for o_ref in (qkv_ref, bz_ref, cz_ref, xbc_ref, sm_ref):
        n = o_ref.shape[-1]
        pieces += [(o_ref, c0, min(c0 + PROJ_PIECE, n), off + c0) for c0 in range(0, n, PROJ_PIECE)]
        off += n
    pieces.append((None, 0, W_GRP, off))
    todo = iter(pieces)

    def project(count=1):
        for _ in range(count):
            nxt = next(todo, None)
            if nxt is None:
                return
            o_ref, c0, c1, woff = nxt
            res = _dot(xn, w_ref[:, woff:woff + c1 - c0])
            if o_ref is None:
                for q in range(S5_OCT):
                    u_ref[q] = res[:, q * LANES:(q + 1) * LANES].reshape(PROJ_TM // S5_SUB, S5_SUB, LANES)
            else:
                o_ref[:, c0:c1] = res

    lru_tile = jnp.maximum(i - 1, 0) % tiles_per_seq
    first = lru_tile == 0
    x_prev = ax_ref[(i + 1) % 2]
    g_prev = ag_ref[(i + 1) % 2]
    carry = jnp.where(first, 0.0, hst_ref[0:1, :])
    prev8 = jnp.where(first, 0.0, prev_ref[...])
    nsub = PROJ_TM // LRU_ROWS
    per_sub = -(-len(pieces) // nsub)
    for c in range(nsub):
        rs = slice(c * LRU_ROWS, (c + 1) * LRU_ROWS)
        y, carry, prev8 = _lru_tile(x_prev[rs], g_prev[rs], prev8, carry, lru_tile * PROJ_TM + c * LRU_ROWS,
                                    cw_ref, cb_ref, wai_ref, bai_ref, lam_ref, ng_ref)
        ya_ref[rs, :] = y.astype(ya_ref.dtype)
        project(per_sub)
    hst_ref[...] = jnp.broadcast_to(carry, hst_ref.shape)
    prev_ref[...] = prev8
    project(len(pieces))


def _inproj_lru(h, l, g, w, cw, cb, wai, bai, lam, ng):
    b, tp, _ = h.shape
    tps = tp // PROJ_TM
    nt = b * tps

    def cur(i):
        t = jnp.minimum(i, nt - 1)
        return t // tps, t % tps

    def lag(i):
        t = jnp.maximum(i - 1, 0)
        return t // tps, t % tps

    seq = lambda ncols, at: pl.BlockSpec((None, PROJ_TM, ncols), lambda i: at(i) + (0,))
    oct_spec = pl.BlockSpec((S5_OCT, PROJ_TM // S5_SUB, None, S5_SUB, LANES),
                            lambda i: (0, cur(i)[1], cur(i)[0], 0, 0))
    return pl.pallas_call(
        functools.partial(_inproj_lru_body, tiles_per_seq=tps),
        grid=(nt + 1,),
        in_specs=[seq(D_MODEL, cur)] + [_layer_spec(p, l) for p in (g, w, cw, cb, wai, bai, lam, ng)],
        out_specs=[seq(n, cur) for n in PROJ_HBM_SLABS] + [oct_spec, seq(W_GRP, lag)],
        out_shape=[jax.ShapeDtypeStruct((b, tp, n), F32) for n in PROJ_HBM_SLABS]
        + [jax.ShapeDtypeStruct((S5_OCT, tp // S5_SUB, b, S5_SUB, LANES), F32),
           jax.ShapeDtypeStruct((b, tp, W_GRP), BF16)],
        scratch_shapes=[pltpu.VMEM((2, PROJ_TM, W_GRP), F32), pltpu.VMEM((2, PROJ_TM, W_GRP), F32),
                        pltpu.VMEM((SUBLANES, W_GRP), F32), pltpu.VMEM((SUBLANES, W_GRP), F32)],
        compiler_params=_compiler_params(("arbitrary",)),
        name="inproj_lru",
    )(h, g, w, cw, cb, wai, bai, lam, ng)


GDN_NC = 11
SSD_NC = 3


def _tri_masks():
    r = _row_iota((CHUNK, CHUNK))
    c = _col_iota((CHUNK, CHUNK))
    return r >= c, r > c


def _chunk_cumsum(x):
    incl, _ = _tri_masks()
    tril = incl.astype(F32)
    triu = (_row_iota((CHUNK, CHUNK)) <= _col_iota((CHUNK, CHUNK))).astype(F32)
    return _dot(tril, x, prec=HI), _dot(x.T, triu, prec=HI)


def _gdn_body(qkv_ref, z_ref, sm_ref, cw_ref, alog_ref, dtb_ref, ng_ref, o_ref, prev_ref, s_ref, *, nc):
    t_idx = pl.program_id(1)
    tt = nc * CHUNK

    @pl.when(t_idx == 0)
    def _():
        prev_ref[...] = jnp.zeros_like(prev_ref)
        s_ref[...] = jnp.zeros_like(s_ref)

    incl, strict = _tri_masks()
    hd = GDN_HEAD_DIM
    x = qkv_ref[...]
    qkv = _silu(_causal_conv4(x, prev_ref[...], cw_ref))
    prev_ref[...] = x[tt - SUBLANES:]
    sm = sm_ref[...]
    beta_all = jax.nn.sigmoid(sm)
    g_all = -jnp.exp(alog_ref[...]) * _softplus(sm + dtb_ref[...])
    g_all = jnp.where(_valid_rows(t_idx * tt, tt, SMALL_W), g_all, 0.0)
    valid = _valid_rows(t_idx * tt, tt, hd)
    cums = [_chunk_cumsum(g_all[c * CHUNK:(c + 1) * CHUNK]) for c in range(nc)]

    heads = range(GDN_HEADS)
    pairs = [(h, c) for h in heads for c in range(nc)]
    rs = [slice(c * CHUNK, (c + 1) * CHUNK) for c in range(nc)]
    qs, ks, kbs, vbs = [], [], [], []
    for h in heads:
        q = qkv[:, h * hd:(h + 1) * hd]
        k = qkv[:, W_GRP + h * hd:W_GRP + (h + 1) * hd]
        v = qkv[:, 2 * W_GRP + h * hd:2 * W_GRP + (h + 1) * hd]
        qs.append(q * lax.rsqrt(jnp.sum(q * q, axis=-1, keepdims=True) + EPS) * (hd ** -0.5))
        k = k * lax.rsqrt(jnp.sum(k * k, axis=-1, keepdims=True) + EPS)
        beta = beta_all[:, BETA_OFF + h:BETA_OFF + h + 1]
        ks.append(k)
        kbs.append(k * beta)
        vbs.append(v * beta)
    gcol = {(h, c): cums[c][0][:, ALPHA_OFF + h:ALPHA_OFF + h + 1] for h, c in pairs}
    grow = {(h, c): cums[c][1][ALPHA_OFF + h:ALPHA_OFF + h + 1, :] for h, c in pairs}
    decay = {p: jnp.exp(jnp.where(incl, gcol[p] - grow[p], -jnp.inf)) for p in pairs}
    eg = {p: jnp.exp(gcol[p]) for p in pairs}
    g_last = {p: gcol[p][CHUNK - 1:CHUNK, :] for p in pairs}

    kk = {(h, c): _bdot(kbs[h][rs[c]], ks[h][rs[c]], NT) for h, c in pairs}
    qk = {(h, c): _bdot(qs[h][rs[c]], ks[h][rs[c]], NT) for h, c in pairs}
    nmat = {p: jnp.where(strict, kk[p] * decay[p], 0.0) for p in pairs}
    attn = {p: qk[p] * decay[p] for p in pairs}
    eye = (_row_iota((CHUNK, CHUNK)) == _col_iota((CHUNK, CHUNK))).astype(F32)
    tinv = {p: eye - nmat[p] for p in pairs}
    npow = {p: _bdot(nmat[p], nmat[p]) for p in pairs}
    span = 2
    while True:
        tinv = {p: tinv[p] + _bdot(tinv[p], npow[p]) for p in pairs}
        span *= 2
        if span >= CHUNK:
            break
        npow = {p: _bdot(npow[p], npow[p]) for p in pairs}
    u = {(h, c): _bdot(tinv[(h, c)], vbs[h][rs[c]]) for h, c in pairs}
    w = {(h, c): _bdot(tinv[(h, c)], kbs[h][rs[c]] * eg[(h, c)]) for h, c in pairs}
    q_dec = {(h, c): qs[h][rs[c]] * eg[(h, c)] for h, c in pairs}
    k_dec_t = {(h, c): (ks[h][rs[c]] * jnp.exp(g_last[(h, c)] - gcol[(h, c)])).T for h, c in pairs}

    s = [s_ref[h] for h in heads]
    outs = [[None] * nc for _ in heads]
    for c in range(nc):
        ws = [_bdot(w[(h, c)], s[h]) for h in heads]
        qd = [_bdot(q_dec[(h, c)], s[h]) for h in heads]
        v_new = [u[(h, c)] - ws[h] for h in heads]
        av = [_bdot(attn[(h, c)], v_new[h]) for h in heads]
        kv = [_bdot(k_dec_t[(h, c)], v_new[h]) for h in heads]
        for h in heads:
            outs[h][c] = qd[h] + av[h]
            s[h] = s[h] * jnp.exp(g_last[(h, c)]) + kv[h]
    for h in heads:
        s_ref[h] = s[h]
        o = jnp.concatenate(outs[h], axis=0)
        o = _rms(o, ng_ref[...]) * _silu(z_ref[:, h * hd:(h + 1) * hd])
        o = jnp.where(valid, o, 0.0)
        o_ref[:, h * hd:(h + 1) * hd] = o.astype(o_ref.dtype)


def _gdn(qkv, z, sm, l, cw, alog, dtb, ng):
    b, tp, _ = qkv.shape
    tt = GDN_NC * CHUNK
    return pl.pallas_call(
        functools.partial(_gdn_body, nc=GDN_NC),
        grid=(b, tp // tt),
        in_specs=[_seq_spec(tt, 3 * W_GRP), _seq_spec(tt, W_GRP), _seq_spec(tt, SMALL_W)]
        + [_layer_spec(p, l) for p in (cw, alog, dtb, ng)],
        out_specs=_seq_spec(tt, W_GRP),
        out_shape=jax.ShapeDtypeStruct((b, tp, W_GRP), BF16),
        scratch_shapes=[pltpu.VMEM((SUBLANES, 3 * W_GRP), F32),
                        pltpu.VMEM((GDN_HEADS, GDN_HEAD_DIM, GDN_HEAD_DIM), F32)],
        compiler_params=_compiler_params(("parallel", "arbitrary")),
        name="gdn",
    )(qkv, z, sm, cw, alog, dtb, ng)


SSD_PAIRS = SSD_HEADS // 2


def _ssd_block(xbc, z, sm, states, row0, alog_ref, dtb_ref, dsk_ref, ng_ref, tick):
    tt = xbc.shape[0]
    nc = tt // CHUNK
    hp = SSD_HEAD_DIM
    gw = W_GRP // SSD_GROUPS
    ppg = SSD_PAIRS // SSD_GROUPS
    expand = (_row_iota((SMALL_W, W_GRP)) - DT_OFF
              == _col_iota((SMALL_W, W_GRP)) >> (hp.bit_length() - 1)).astype(F32)
    lane = _col_iota((CHUNK, LANES))
    lo_half = lane < hp
    incl2 = _row_iota((CHUNK, LANES)) >= (lane & (hp - 1))
    lo_rows = _row_iota((LANES, 1)) < hp

    xs = xbc[:, :W_GRP]
    dt_all = _softplus(sm + dtb_ref[...])
    dt_all = jnp.where(_valid_rows(row0, tt, SMALL_W), dt_all, 0.0)
    adt_all = dt_all * -jnp.exp(alog_ref[...])
    x_in = xs * _dot(dt_all, expand, prec=HI)
    cums = [_chunk_cumsum(adt_all[c * CHUNK:(c + 1) * CHUNK]) for c in range(nc)]
    rs = [slice(c * CHUNK, (c + 1) * CHUNK) for c in range(nc)]
    pairs = [(j, c) for j in range(SSD_PAIRS) for c in range(nc)]
    grp = lambda j: j // ppg

    def head_cols(j, c):
        cum, cum_t = cums[c]
        h0 = DT_OFF + 2 * j
        acol = jnp.where(lo_half, cum[:, h0:h0 + 1], cum[:, h0 + 1:h0 + 2])
        arow = jnp.concatenate([cum_t[h0:h0 + 1, :], cum_t[h0 + 1:h0 + 2, :]], axis=1)
        return acol, arow

    b_all = [xbc[:, W_GRP + g * SSD_STATE:W_GRP + (g + 1) * SSD_STATE] for g in range(SSD_GROUPS)]
    c_all = [xbc[:, W_GRP + (SSD_GROUPS + g) * SSD_STATE:W_GRP + (SSD_GROUPS + g + 1) * SSD_STATE]
             for g in range(SSD_GROUPS)]
    cols = {p: head_cols(*p) for p in pairs}
    a_last = {p: cols[p][0][CHUNK - 1:CHUNK, :] for p in pairs}
    xp = {(j, c): x_in[rs[c], j * LANES:(j + 1) * LANES] for j, c in pairs}

    cb = {(g, c): _bdot(c_all[g][rs[c]], b_all[g][rs[c]], NT) for g in range(SSD_GROUPS) for c in range(nc)}
    cb2 = {k: jnp.concatenate([v, v], axis=1) for k, v in cb.items()}
    lm = {p: cb2[(grp(p[0]), p[1])] * jnp.exp(jnp.where(incl2, cols[p][0] - cols[p][1], -jnp.inf))
          for p in pairs}
    x_bd = {p: jnp.concatenate([jnp.where(lo_half, xp[p], 0.0), jnp.where(lo_half, 0.0, xp[p])], axis=0)
            for p in pairs}
    y_diag = {p: _bdot(lm[p], x_bd[p]) for p in pairs}
    tick()
    st = {(j, c): _bdot((xp[(j, c)] * jnp.exp(a_last[(j, c)] - cols[(j, c)][0])).T, b_all[grp(j)][rs[c]])
          for j, c in pairs}
    tick()
    s_in = {}
    new_states = []
    for j in range(SSD_PAIRS):
        s = states[j]
        for c in range(nc):
            s_in[(j, c)] = s
            e_last = jnp.exp(a_last[(j, c)])
            s = s * jnp.where(lo_rows, e_last[:, 0:1], e_last[:, hp:hp + 1]) + st[(j, c)]
        new_states.append(s)
    y = {(j, c): y_diag[(j, c)] + _bdot(c_all[grp(j)][rs[c]], s_in[(j, c)], NT) * jnp.exp(cols[(j, c)][0])
         for j, c in pairs}
    tick()
    y = jnp.concatenate([jnp.concatenate([y[(j, c)] for j in range(SSD_PAIRS)], axis=1)
                         for c in range(nc)], axis=0)
    y = (y + dsk_ref[...] * xs) * _silu(z)
    y = jnp.concatenate(
        [_rms(y[:, g * gw:(g + 1) * gw], ng_ref[:, g * gw:(g + 1) * gw]) for g in range(SSD_GROUPS)],
        axis=1)
    return jnp.where(_valid_rows(row0, tt, W_GRP), y, 0.0), new_states


OUT_TM = 704
OUT_PIECE = 256


def _ssd_outproj_body(z_ref, xbc_ref, sm_ref, cw_ref, cb_ref, alog_ref, dtb_ref, dsk_ref, ng_ref,
                      h_ref, ya_ref, yb_ref, yd_ref, ud_ref, s5d_ref, wglu_ref, s5ng_ref, wo_ref,
                      o_ref, yc_ref, prev_ref, s_ref, *, tiles_per_seq):
    i = pl.program_id(0)
    last = pl.num_programs(0) - 2

    @pl.when(i == 0)
    def _():
        yc_ref[...] = jnp.zeros_like(yc_ref)
        prev_ref[...] = jnp.zeros_like(prev_ref)
        s_ref[...] = jnp.zeros_like(s_ref)

    yc_prev = yc_ref.at[(i + 1) % 2]
    yc_cur = yc_ref.at[i % 2]
    tok = lambda ref: jnp.concatenate([ref[q].reshape(OUT_TM, LANES) for q in range(S5_OCT)], axis=1)
    acc = {}

    def s5_epilogue():
        yd = tok(yd_ref) + s5d_ref[...] * tok(ud_ref)
        yd = _gelu_tanh(yd)
        yd = yd * jax.nn.sigmoid(_dot(yd.astype(BF16), wglu_ref[...]))
        acc["yd"] = _rms(yd, s5ng_ref[...]).astype(BF16)

    def proj_piece(col, grp):
        cols = slice(col * OUT_PIECE, (col + 1) * OUT_PIECE)
        rows = slice(grp * W_GRP, (grp + 1) * W_GRP)
        lhs = (ya_ref, yb_ref, yc_prev, None)[grp]
        lhs = acc["yd"] if lhs is None else lhs[...]
        part = _dot(lhs, wo_ref[rows, cols])
        if grp == 0:
            acc[col] = h_ref[:, cols] + part
        elif grp < 3:
            acc[col] = acc[col] + part
        else:
            o_ref[:, cols] = acc[col] + part

    pieces = [s5_epilogue] + [functools.partial(proj_piece, col, grp)
                              for col in range(D_MODEL // OUT_PIECE) for grp in range(4)]
    todo = iter(pieces)

    def tick():
        nxt = next(todo, None)
        if nxt is not None:
            nxt()

    tile = jnp.minimum(i, last) % tiles_per_seq
    first = tile == 0
    prev8 = jnp.where(first, 0.0, prev_ref[...])
    states = [jnp.where(first, 0.0, s_ref[j]) for j in range(SSD_PAIRS)]
    nchunks = OUT_TM // CHUNK
    blocks = [slice(c0 * CHUNK, min(c0 + SSD_NC, nchunks) * CHUNK) for c0 in range(0, nchunks, SSD_NC)]
    for rs in blocks:
        x = xbc_ref[rs, :]
        act = _silu(_causal_conv4(x, prev8, cw_ref) + cb_ref[...])
        prev8 = x[x.shape[0] - SUBLANES:]
        tick()
        y, states = _ssd_block(act, z_ref[rs, :], sm_ref[rs, :], states, tile * OUT_TM + rs.start,
                               alog_ref, dtb_ref, dsk_ref, ng_ref, tick)
        yc_cur[rs, :] = y.astype(yc_cur.dtype)
        tick()
    for _ in pieces:
        tick()
    prev_ref[...] = prev8
    for j in range(SSD_PAIRS):
        s_ref[j] = states[j]


def _ssd_outproj(z, xbc, sm, h, ya, yb, yd, ud, l, ssd_p, out_p):
    b, tp, _ = h.shape
    tps = tp // OUT_TM
    nt = b * tps

    def cur(i):
        t = jnp.minimum(i, nt - 1)
        return t // tps, t % tps

    def lag(i):
        t = jnp.maximum(i - 1, 0)
        return t // tps, t % tps

    seq = lambda ncols, at: pl.BlockSpec((None, OUT_TM, ncols), lambda i: at(i) + (0,))
    oct_spec = pl.BlockSpec((S5_OCT, OUT_TM // S5_SUB, None, S5_SUB, LANES),
                            lambda i: (0, lag(i)[1], lag(i)[0], 0, 0))
    return pl.pallas_call(
        functools.partial(_ssd_outproj_body, tiles_per_seq=tps),
        grid=(nt + 1,),
        in_specs=[seq(W_GRP, cur), seq(XBC_W, cur), seq(SMALL_W, cur)] + [_layer_spec(p, l) for p in ssd_p]
        + [seq(D_MODEL, lag), seq(W_GRP, lag), seq(W_GRP, lag), oct_spec, oct_spec]
        + [_layer_spec(p, l) for p in out_p],
        out_specs=seq(D_MODEL, lag),
        out_shape=jax.ShapeDtypeStruct(h.shape, F32),
        scratch_shapes=[pltpu.VMEM((2, OUT_TM, W_GRP), BF16), pltpu.VMEM((SUBLANES, XBC_W), F32),
                        pltpu.VMEM((SSD_PAIRS, 2 * SSD_HEAD_DIM, SSD_STATE), F32)],
        compiler_params=_compiler_params(("arbitrary",)),
        name="ssd_outproj",
    )(z, xbc, sm, *ssd_p, h, ya, yb, yd, ud, *out_p)


def _s5_discretise(a_re, a_im, log_dt):
    lam_re = jnp.minimum(a_re, -1e-4)
    lam_im = a_im
    dt = jnp.exp(log_dt)
    mag = jnp.exp(dt * lam_re)
    ab_re = mag * jnp.cos(dt * lam_im)
    ab_im = mag * jnp.sin(dt * lam_im)
    den = lam_re * lam_re + lam_im * lam_im
    f_re = ((ab_re - 1.0) * lam_re + ab_im * lam_im) / den
    f_im = (ab_im * lam_re - (ab_re - 1.0) * lam_im) / den
    return ab_re, ab_im, f_re, f_im


def _s5_prep_body(ar_ref, ai_ref, ld_ref, arc_ref, aic_ref, ldc_ref, btr_ref, bti_ref, ctr_ref, cti_ref,
                  w1_ref, wc_ref, apow_ref):
    ab_re, ab_im, f_re, f_im = _s5_discretise(ar_ref[...], ai_ref[...], ld_ref[...])
    abc_re, abc_im, _, _ = _s5_discretise(arc_ref[...], aic_ref[...], ldc_ref[...])
    bb_re = f_re * btr_ref[...] - f_im * bti_ref[...]
    bb_im = f_re * bti_ref[...] + f_im * btr_ref[...]
    ct_re = ctr_ref[...]
    ct_im = cti_ref[...]

    pows = [(jnp.ones_like(ab_re), jnp.zeros_like(ab_re))]
    for _ in range(S5_SUB):
        p_re, p_im = pows[-1]
        pows.append((p_re * ab_re - p_im * ab_im, p_re * ab_im + p_im * ab_re))
    x_re = jnp.concatenate([bb_re * pows[S5_SUB - 1 - s][0] - bb_im * pows[S5_SUB - 1 - s][1]
                            for s in range(S5_SUB)], axis=0)
    x_im = jnp.concatenate([bb_re * pows[S5_SUB - 1 - s][1] + bb_im * pows[S5_SUB - 1 - s][0]
                            for s in range(S5_SUB)], axis=0)
    taps = _dot(x_re, ct_re, prec=HI) - _dot(x_im, ct_im, prec=HI)
    cols = []
    for l in range(S5_SUB):
        up = (S5_SUB - 1 - l) * LANES
        cols.append(taps[up:] if up == 0 else
                    jnp.concatenate([taps[up:], jnp.zeros((up, LANES), F32)], axis=0))
    w1_ref[...] = jnp.concatenate(cols + [x_re, x_im], axis=1).astype(w1_ref.dtype)
    apow_ref[...] = jnp.concatenate(pows[S5_SUB], axis=1)

    q_re, q_im = abc_re, abc_im
    wc_re, wc_im = [], []
    for l in range(S5_SUB):
        wc_re.append(ct_re * q_re - ct_im * q_im)
        wc_im.append(-(ct_re * q_im + ct_im * q_re))
        q_re, q_im = q_re * abc_re - q_im * abc_im, q_re * abc_im + q_im * abc_re
    wc_ref[...] = jnp.concatenate([jnp.concatenate(wc_re, axis=1), jnp.concatenate(wc_im, axis=1)],
                                  axis=0).astype(wc_ref.dtype)


def _block_diag(w):
    s, i, j = w.shape[-3:]
    eye = jnp.eye(s, dtype=w.dtype)
    return (w[..., :, :, None, :] * eye[:, None, :, None]).reshape(w.shape[:-3] + (s * i, s * j))


def _s5_prep(a_re, a_im, log_dt, b_re, b_im, c_re, c_im):
    depth = a_re.shape[0]
    row = lambda v: v.reshape(depth, S5_OCT, 1, S5_NST)
    col = lambda v: v.reshape(depth, S5_OCT, S5_NST, 1)
    ld = jnp.repeat(log_dt, S5_STATE, axis=1)
    octs = lambda w: _block_diag(jnp.swapaxes(w, 2, 3).reshape(
        (depth, S5_OCT, S5_OCT_GROUPS) + (w.shape[3], w.shape[2])))

    def ospec(shape):
        return pl.BlockSpec((None, None) + shape, lambda d, q: (d, q) + (0,) * len(shape))

    outs = [((S5_VEC, S5_VEC + 2 * S5_NST), BF16), ((2 * S5_NST, S5_VEC), BF16), ((1, 2 * S5_NST), F32)]
    return pl.pallas_call(
        _s5_prep_body,
        grid=(depth, S5_OCT),
        in_specs=[ospec((1, S5_NST))] * 3 + [ospec((S5_NST, 1))] * 3
        + [ospec((LANES, S5_NST))] * 2 + [ospec((S5_NST, LANES))] * 2,
        out_specs=[ospec(s) for s, _ in outs],
        out_shape=[jax.ShapeDtypeStruct((depth, S5_OCT) + s, d) for s, d in outs],
        compiler_params=_compiler_params(("parallel", "parallel")),
        name="s5_prep",
    )(row(a_re), row(a_im), row(ld), col(a_re), col(a_im), col(ld),
      octs(b_re), octs(b_im), octs(c_re), octs(c_im))


S5_RT = 1056
S5_SCAN_UNROLL = 4


def _s5_front(u_ref, w1_ref, loc_ref):
    u = jnp.concatenate([u_ref[pl.ds(s, S5_RT, stride=S5_SUB), :] for s in range(S5_SUB)], axis=1)
    z = _dot(u.astype(BF16), w1_ref[...])
    loc_ref[...] = z[:, S5_VEC:]
    return z[:, :S5_VEC]


def _s5_back(y_intra, wc_ref, apow_ref, y_ref, loc_ref, sin_ref, st_ref):
    nb = st_ref.shape[0]
    a_re = apow_ref[:, :S5_NST]
    a_im = apow_ref[:, S5_NST:]

    def step(c, carry):
        s_re, s_im = carry
        r0 = pl.multiple_of(c * nb, nb)
        sin_ref[pl.ds(r0, nb), :S5_NST] = s_re
        sin_ref[pl.ds(r0, nb), S5_NST:] = s_im
        n_re = a_re * s_re - a_im * s_im + loc_ref[pl.ds(r0, nb), :S5_NST]
        n_im = a_re * s_im + a_im * s_re + loc_ref[pl.ds(r0, nb), S5_NST:]
        return n_re, n_im

    s_re, s_im = lax.fori_loop(0, S5_RT // nb, step, (st_ref[:, :S5_NST], st_ref[:, S5_NST:]),
                               unroll=S5_SCAN_UNROLL)
    st_ref[:, :S5_NST] = s_re
    st_ref[:, S5_NST:] = s_im
    y = y_intra + _dot(sin_ref[...].astype(BF16), wc_ref[...])
    for l in range(S5_SUB):
        y_ref[pl.ds(l, S5_RT, stride=S5_SUB), :] = y[:, l * LANES:(l + 1) * LANES]


def _s5_body(u_ref, w1_ref, wc_ref, apow_ref, y_ref, loc_ref, sin_ref, st_ref):
    @pl.when(pl.program_id(1) == 0)
    def _():
        st_ref[...] = jnp.zeros_like(st_ref)

    y_intra = _s5_front(u_ref, w1_ref, loc_ref)
    _s5_back(y_intra, wc_ref, apow_ref, y_ref, loc_ref, sin_ref, st_ref)


def _s5_scan(u, l, w1, wc, apow, nb):
    noct, rows, _ = u.shape

    def wspec(arr):
        shape = arr.shape[2:]
        return pl.BlockSpec((None, None) + shape, lambda q, r: (l, q) + (0,) * len(shape))

    tile = pl.BlockSpec((None, S5_RT * S5_SUB, LANES), lambda q, r: (q, r, 0))
    return pl.pallas_call(
        _s5_body,
        grid=(noct, rows // (S5_RT * S5_SUB)),
        in_specs=[tile, wspec(w1), wspec(wc), wspec(apow)],
        out_specs=tile,
        out_shape=jax.ShapeDtypeStruct(u.shape, F32),
        scratch_shapes=[pltpu.VMEM((S5_RT, 2 * S5_NST), F32), pltpu.VMEM((S5_RT, 2 * S5_NST), F32),
                        pltpu.VMEM((nb, 2 * S5_NST), F32)],
        compiler_params=_compiler_params(("parallel", "arbitrary")),
        name="s5_scan",
    )(u, w1, wc, apow)


def _final_norm_body(h_ref, g_ref, o_ref):
    skip = h_ref.shape[0] - o_ref.shape[0]
    o_ref[...] = _rms(h_ref[skip:, :], g_ref[...])


def _final_norm(h, g, seq):
    b, tp, d = h.shape
    return pl.pallas_call(
        _final_norm_body,
        grid=(b,),
        in_specs=[pl.BlockSpec((None, tp, d), lambda i: (i, 0, 0)), pl.BlockSpec((1, d), lambda i: (0, 0))],
        out_specs=pl.BlockSpec((None, seq, d), lambda i: (i, 0, 0)),
        out_shape=jax.ShapeDtypeStruct((b, seq, d), F32),
        compiler_params=_compiler_params(("parallel",)),
        name="final_norm",
    )(h, g)


def _small_rows(vals, off):
    depth, n = vals.shape
    return jnp.zeros((depth, 1, SMALL_W), F32).at[:, 0, off:off + n].set(vals)


def _pack_w_in(w_in):
    sizes = (W_GRP, W_GRP, 3 * W_GRP, W_GRP, GDN_HEADS, GDN_HEADS, W_GRP, XBC_W, SSD_HEADS, W_GRP)
    offs = np.concatenate([[0], np.cumsum(sizes)])
    a_x, a_gate, b_qkv, b_z, b_beta, b_alpha, c_z, c_xbc, c_dt, d_u = (
        w_in[..., offs[i]:offs[i + 1]] for i in range(len(sizes)))
    pad = lambda n: jnp.zeros(w_in.shape[:-1] + (n,), w_in.dtype)
    small = jnp.concatenate([b_beta, b_alpha, c_dt, pad(SMALL_W - DT_OFF - SSD_HEADS)], axis=-1)
    return jnp.concatenate([a_x, a_gate, b_qkv, b_z, c_z, c_xbc, small, d_u], axis=-1)


def kernel(x, meta_tokens, ffn1_norm, ffn1_w_gate, ffn1_w_up, ffn1_w_down, mix_norm, w_in, w_out,
           lru_conv_w, lru_conv_b, lru_w_a, lru_b_a, lru_w_i, lru_b_i, lru_lambda, lru_norm,
           gdn_conv_w, gdn_a_log, gdn_dt_bias, gdn_norm,
           ssd_conv_w, ssd_conv_b, ssd_a_log, ssd_dt_bias, ssd_d, ssd_norm,
           s5_a_re, s5_a_im, s5_log_dt, s5_b_re, s5_b_im, s5_c_re, s5_c_im, s5_d, s5_w_glu, s5_norm,
           ffn2_norm, ffn2_w_gate, ffn2_w_up, ffn2_w_down, final_norm):
    bsz, seq, d = x.shape
    depth = w_in.shape[0]
    tp = LEAD_PAD + N_META + seq
    m = bsz * tp
    nsub = tp // S5_SUB
    assert d == D_MODEL and bsz == SUBLANES and m % FFN_TM == 0
    assert all(tp % t == 0 for t in (PROJ_TM, GDN_NC * CHUNK, OUT_TM))
    assert (nsub * bsz) % S5_RT == 0
    assert (BETA_OFF, ALPHA_OFF, DT_OFF) == (0, GDN_HEADS, 2 * GDN_HEADS)

    meta = jnp.broadcast_to(meta_tokens.astype(x.dtype)[None], (bsz, N_META, d))
    h = jnp.concatenate([jnp.zeros((bsz, LEAD_PAD, d), x.dtype), meta, x], axis=1)

    rows = lambda v: v.reshape(depth, 1, -1).astype(F32)
    bf = lambda w: w.astype(BF16)
    ffn1 = (rows(ffn1_norm), bf(ffn1_w_gate), bf(ffn1_w_up), bf(ffn1_w_down))
    ffn2 = (rows(ffn2_norm), bf(ffn2_w_gate), bf(ffn2_w_up), bf(ffn2_w_down))
    proj = (rows(mix_norm), _pack_w_in(bf(w_in)))
    lru_p = (lru_conv_w, rows(lru_conv_b),
             jnp.concatenate([_block_diag(bf(lru_w_a)), _block_diag(bf(lru_w_i))], axis=-1),
             rows(jnp.concatenate([lru_b_a, lru_b_i], axis=-1)), rows(lru_lambda), rows(lru_norm))
    gdn_p = (gdn_conv_w, _small_rows(gdn_a_log, ALPHA_OFF), _small_rows(gdn_dt_bias, ALPHA_OFF), rows(gdn_norm))
    ssd_p = (ssd_conv_w, rows(ssd_conv_b), _small_rows(ssd_a_log, DT_OFF), _small_rows(ssd_dt_bias, DT_OFF),
             rows(jnp.repeat(ssd_d, SSD_HEAD_DIM, axis=1)), rows(ssd_norm))
    s5_w = _s5_prep(s5_a_re, s5_a_im, s5_log_dt, s5_b_re, s5_b_im, s5_c_re, s5_c_im)
    out_p = (rows(s5_d), bf(s5_w_glu), rows(s5_norm), bf(w_out))

    flat = lambda t: t.reshape(m, d)
    seq3 = lambda t: t.reshape(bsz, tp, d)
    for l in range(depth):
        h = seq3(_ffn(flat(h), l, *ffn1))
        b_qkv, b_z, c_z, c_xbc, small, d_u, y_a = _inproj_lru(h, l, *proj, *lru_p)
        y_b = _gdn(b_qkv, b_z, small, l, *gdn_p)
        y_d = _s5_scan(d_u.reshape(S5_OCT, nsub * bsz * S5_SUB, LANES), l, *s5_w, bsz).reshape(d_u.shape)
        h = _ssd_outproj(c_z, c_xbc, small, h, y_a, y_b, y_d, d_u, l, ssd_p, out_p)
        h = seq3(_ffn(flat(h), l, *ffn2))

    return _final_norm(h, final_norm.reshape(1, d), seq)
```

```python
import functools

import numpy as np
import jax
import jax.numpy as jnp
from jax import lax
from jax.experimental import pallas as pl
from jax.experimental.pallas import tpu as pltpu

F32 = jnp.float32
BF16 = jnp.bfloat16
HI = lax.Precision.HIGHEST

D_MODEL = 1024
N_META = 16
W_GRP = 512
CONV_K = 4
D_FF = 2816
EPS = 1e-6
CHUNK = 64
LEAD_PAD = CHUNK - N_META
LRU_C = 8.0
GDN_HEADS = 4
GDN_HEAD_DIM = 128
SSD_HEADS = 8
SSD_HEAD_DIM = 64
SSD_GROUPS = 2
SSD_STATE = 128
XBC_W = W_GRP + 2 * SSD_GROUPS * SSD_STATE
S5_GROUP_CH = 16
S5_GROUPS = 32
S5_STATE = 64

LANES = 128
SUBLANES = 8
SMALL_W = LANES
BETA_OFF, ALPHA_OFF, DT_OFF = 0, 4, 8
VMEM_LIMIT = 56 * 1024 * 1024

S5_SUB = SUBLANES
S5_OCT = W_GRP // LANES
S5_OCT_GROUPS = S5_GROUPS // S5_OCT
S5_VEC = S5_SUB * LANES
S5_NST = S5_OCT_GROUPS * S5_STATE

NN = (((1,), (0,)), ((), ()))
NT = (((1,), (1,)), ((), ()))


def _dot(a, b, dims=NN, prec=None):
    return lax.dot_general(a, b, dims, precision=prec, preferred_element_type=F32)


def _bdot(a, b, dims=NN):
    return lax.dot_general(a.astype(BF16), b.astype(BF16), dims, preferred_element_type=F32)


def _split_bf16(x):
    hi = x.astype(BF16)
    return hi, (x - hi.astype(F32)).astype(BF16)


def _dot_split(a, b):
    return _dot(a[0], b[0]) + (_dot(a[0], b[1]) + _dot(a[1], b[0]))


def _rms(x, g):
    return x * lax.rsqrt(jnp.mean(x * x, axis=-1, keepdims=True) + EPS) * g


def _silu(x):
    return x * jax.nn.sigmoid(x)


def _gelu_tanh(x):
    return 0.5 * x * (1.0 + jnp.tanh(np.sqrt(2.0 / np.pi).astype(np.float32) * (x + 0.044715 * (x * x * x))))


def _softplus(x):
    return jnp.maximum(x, 0.0) + jnp.log1p(jnp.exp(-jnp.abs(x)))


def _row_iota(shape):
    return lax.broadcasted_iota(jnp.int32, shape, 0)


def _col_iota(shape):
    return lax.broadcasted_iota(jnp.int32, shape, 1)


def _roll_groups(x, k):
    r, c = x.shape
    return pltpu.roll(x.reshape(r // SUBLANES, SUBLANES, c), k, axis=1).reshape(r, c)


def _shift_rows(x, prev8, k):
    xs = pltpu.roll(x, k, axis=0)
    ps = pltpu.roll(prev8, k, axis=0)
    top = jnp.where(_row_iota(ps.shape) < k, ps, xs[:SUBLANES])
    return jnp.concatenate([top, xs[SUBLANES:]], axis=0)


def _causal_conv4(x, prev8, w_ref):
    y = x * w_ref[CONV_K - 1:CONV_K, :]
    for k in range(1, CONV_K):
        y = y + _shift_rows(x, prev8, k) * w_ref[CONV_K - 1 - k:CONV_K - k, :]
    return y


def _compiler_params(sem):
    return pltpu.CompilerParams(dimension_semantics=sem, vmem_limit_bytes=VMEM_LIMIT)


def _layer_spec(arr, l):
    shape = arr.shape[1:]
    return pl.BlockSpec((None,) + shape, lambda *_: (l,) + (0,) * len(shape),
                        pipeline_mode=pl.Buffered(1))


def _valid_rows(row0, nrows, ncols):
    return _row_iota((nrows, ncols)) + row0 >= LEAD_PAD


def _seq_spec(tt, ncols):
    return pl.BlockSpec((None, tt, ncols), lambda b, t: (b, t, 0))


FFN_TM = 1056
FFN_FC = 256


def _ffn_body(h_ref, g_ref, wg_ref, wu_ref, wd_ref, o_ref):
    h = h_ref[...]
    xn = _rms(h, g_ref[...]).astype(BF16)
    acc = jnp.zeros_like(h)
    for c in range(D_FF // FFN_FC):
        sl = slice(c * FFN_FC, (c + 1) * FFN_FC)
        gt = _dot(xn, wg_ref[:, sl])
        up = _dot(xn, wu_ref[:, sl])
        acc = acc + _dot((_silu(gt) * up).astype(BF16), wd_ref[sl, :])
    o_ref[...] = h + 0.5 * acc


def _ffn(h, l, g, wg, wu, wd):
    m = h.shape[0]
    tok = pl.BlockSpec((FFN_TM, D_MODEL), lambda i: (i, 0))
    return pl.pallas_call(
        _ffn_body,
        grid=(m // FFN_TM,),
        in_specs=[tok] + [_layer_spec(p, l) for p in (g, wg, wu, wd)],
        out_specs=tok,
        out_shape=jax.ShapeDtypeStruct(h.shape, F32),
        compiler_params=_compiler_params(("parallel",)),
        name="ffn",
    )(h, g, wg, wu, wd)


PROJ_TM = 704
PROJ_HBM_SLABS = (3 * W_GRP, W_GRP, W_GRP, XBC_W, SMALL_W)
PROJ_W = 2 * W_GRP + sum(PROJ_HBM_SLABS) + W_GRP
PROJ_PIECE = 512
LRU_ROWS = 176


def _lru_tile(x, ug, prev8, carry_row, row0, cw_ref, cb_ref, wai_ref, bai_ref, lam_ref, ng_ref):
    n = x.shape[0]
    xc = _causal_conv4(x, prev8, cw_ref) + cb_ref[...]
    gates = _dot(xc.astype(BF16), wai_ref[...]) + bai_ref[...]
    r = jax.nn.sigmoid(gates[:, :W_GRP])
    ig = jax.nn.sigmoid(gates[:, W_GRP:])
    log_a = (-LRU_C * _softplus(-lam_ref[...])) * r
    a = jnp.exp(log_a)
    valid = _valid_rows(row0, n, W_GRP)
    bv = jnp.sqrt(jnp.tanh(-log_a) * (1.0 + a * a)) * (ig * xc)
    bv = jnp.where(valid, bv, 0.0)
    sub = _row_iota((n, W_GRP)) & (SUBLANES - 1)
    s = 1
    while s < SUBLANES:
        a_sh = jnp.where(sub < s, 1.0, _roll_groups(a, s))
        b_sh = jnp.where(sub < s, 0.0, _roll_groups(bv, s))
        bv = a * b_sh + bv
        a = a * a_sh
        s *= 2
    groups = []
    for g in range(n // SUBLANES):
        rows = slice(g * SUBLANES, (g + 1) * SUBLANES)
        hg = bv[rows] + a[rows] * carry_row
        carry_row = hg[SUBLANES - 1:SUBLANES, :]
        groups.append(hg)
    y = _rms(_gelu_tanh(ug) * jnp.concatenate(groups, axis=0), ng_ref[...])
    return jnp.where(valid, y, 0.0), carry_row, x[n - SUBLANES:]


def _inproj_lru_body(h_ref, g_ref, w_ref, cw_ref, cb_ref, wai_ref, bai_ref, lam_ref, ng_ref,
                     qkv_ref, bz_ref, cz_ref, xbc_ref, sm_ref, u_ref, ya_ref,
                     ax_ref, ag_ref, prev_ref, hst_ref, *, tiles_per_seq):
    i = pl.program_id(0)

    @pl.when(i == 0)
    def _():
        ax_ref[...] = jnp.zeros_like(ax_ref)
        ag_ref[...] = jnp.zeros_like(ag_ref)
        prev_ref[...] = jnp.zeros_like(prev_ref)
        hst_ref[...] = jnp.zeros_like(hst_ref)

    xn = _rms(h_ref[...], g_ref[...]).astype(BF16)
    pieces = [(ax_ref.at[i % 2], 0, W_GRP, 0), (ag_ref.at[i % 2], 0, W_GRP, W_GRP)]
    off = 2 * W_GRP
    for o_ref in (qkv_ref, bz_ref, cz_ref, xbc_ref, sm_ref):
        n = o_ref.shape[-1]
        pieces += [(o_ref, c0, min(c0 + PROJ_PIECE, n), off + c0) for c0 in range(0, n, PROJ_PIECE)]
        off += n
    pieces.append((None, 0, W_GRP, off))
    todo = iter(pieces)

    def project(count=1):
        for _ in range(count):
            nxt = next(todo, None)
            if nxt is None:
                return
            o_ref, c0, c1, woff = nxt
            res = _dot(xn, w_ref[:, woff:woff + c1 - c0])
            if o_ref is None:
                for q in range(S5_OCT):
                    u_ref[q] = res[:, q * LANES:(q + 1) * LANES].reshape(PROJ_TM // S5_SUB, S5_SUB, LANES)
            else:
                o_ref[:, c0:c1] = res

    lru_tile = jnp.maximum(i - 1, 0) % tiles_per_seq
    first = lru_tile == 0
    x_prev = ax_ref[(i + 1) % 2]
    g_prev = ag_ref[(i + 1) % 2]
    carry = jnp.where(first, 0.0, hst_ref[0:1, :])
    prev8 = jnp.where(first, 0.0, prev_ref[...])
    nsub = PROJ_TM // LRU_ROWS
    per_sub = -(-len(pieces) // nsub)
    for c in range(nsub):
        rs = slice(c * LRU_ROWS, (c + 1) * LRU_ROWS)
        y, carry, prev8 = _lru_tile(x_prev[rs], g_prev[rs], prev8, carry, lru_tile * PROJ_TM + c * LRU_ROWS,
                                    cw_ref, cb_ref, wai_ref, bai_ref, lam_ref, ng_ref)
        ya_ref[rs, :] = y.astype(ya_ref.dtype)
        project(per_sub)
    hst_ref[...] = jnp.broadcast_to(carry, hst_ref.shape)
    prev_ref[...] = prev8
    project(len(pieces))


def _inproj_lru(h, l, g, w, cw, cb, wai, bai, lam, ng):
    b, tp, _ = h.shape
    tps = tp // PROJ_TM
    nt = b * tps

    def cur(i):
        t = jnp.minimum(i, nt - 1)
        return t // tps, t % tps

    def lag(i):
        t = jnp.maximum(i - 1, 0)
        return t // tps, t % tps

    seq = lambda ncols, at: pl.BlockSpec((None, PROJ_TM, ncols), lambda i: at(i) + (0,))
    oct_spec = pl.BlockSpec((S5_OCT, PROJ_TM // S5_SUB, None, S5_SUB, LANES),
                            lambda i: (0, cur(i)[1], cur(i)[0], 0, 0))
    return pl.pallas_call(
        functools.partial(_inproj_lru_body, tiles_per_seq=tps),
        grid=(nt + 1,),
        in_specs=[seq(D_MODEL, cur)] + [_layer_spec(p, l) for p in (g, w, cw, cb, wai, bai, lam, ng)],
        out_specs=[seq(n, cur) for n in PROJ_HBM_SLABS] + [oct_spec, seq(W_GRP, lag)],
        out_shape=[jax.ShapeDtypeStruct((b, tp, n), F32) for n in PROJ_HBM_SLABS]
        + [jax.ShapeDtypeStruct((S5_OCT, tp // S5_SUB, b, S5_SUB, LANES), F32),
           jax.ShapeDtypeStruct((b, tp, W_GRP), BF16)],
        scratch_shapes=[pltpu.VMEM((2, PROJ_TM, W_GRP), F32), pltpu.VMEM((2, PROJ_TM, W_GRP), F32),
                        pltpu.VMEM((SUBLANES, W_GRP), F32), pltpu.VMEM((SUBLANES, W_GRP), F32)],
        compiler_params=_compiler_params(("arbitrary",)),
        name="inproj_lru",
    )(h, g, w, cw, cb, wai, bai, lam, ng)


GDN_NC = 11
SSD_NC = 3


def _tri_masks():
    r = _row_iota((CHUNK, CHUNK))
    c = _col_iota((CHUNK, CHUNK))
    return r >= c, r > c


def _chunk_cumsum(x):
    incl, _ = _tri_masks()
    tril = incl.astype(F32)
    triu = (_row_iota((CHUNK, CHUNK)) <= _col_iota((CHUNK, CHUNK))).astype(F32)
    return _dot(tril, x, prec=HI), _dot(x.T, triu, prec=HI)


def _gdn_body(qkv_ref, z_ref, sm_ref, cw_ref, alog_ref, dtb_ref, ng_ref, o_ref, prev_ref, s_ref, *, nc):
    t_idx = pl.program_id(1)
    tt = nc * CHUNK

    @pl.when(t_idx == 0)
    def _():
        prev_ref[...] = jnp.zeros_like(prev_ref)
        s_ref[...] = jnp.zeros_like(s_ref)

    incl, strict = _tri_masks()
    hd = GDN_HEAD_DIM
    x = qkv_ref[...]
    qkv = _silu(_causal_conv4(x, prev_ref[...], cw_ref))
    prev_ref[...] = x[tt - SUBLANES:]
    sm = sm_ref[...]
    beta_all = jax.nn.sigmoid(sm)
    g_all = -jnp.exp(alog_ref[...]) * _softplus(sm + dtb_ref[...])
    g_all = jnp.where(_valid_rows(t_idx * tt, tt, SMALL_W), g_all, 0.0)
    valid = _valid_rows(t_idx * tt, tt, hd)
    cums = [_chunk_cumsum(g_all[c * CHUNK:(c + 1) * CHUNK]) for c in range(nc)]

    heads = range(GDN_HEADS)
    pairs = [(h, c) for h in heads for c in range(nc)]
    rs = [slice(c * CHUNK, (c + 1) * CHUNK) for c in range(nc)]
    qs, ks, kbs, vbs = [], [], [], []
    for h in heads:
        q = qkv[:, h * hd:(h + 1) * hd]
        k = qkv[:, W_GRP + h * hd:W_GRP + (h + 1) * hd]
        v = qkv[:, 2 * W_GRP + h * hd:2 * W_GRP + (h + 1) * hd]
        qs.append(q * lax.rsqrt(jnp.sum(q * q, axis=-1, keepdims=True) + EPS) * (hd ** -0.5))
        k = k * lax.rsqrt(jnp.sum(k * k, axis=-1, keepdims=True) + EPS)
        beta = beta_all[:, BETA_OFF + h:BETA_OFF + h + 1]
        ks.append(k)
        kbs.append(k * beta)
        vbs.append(v * beta)
    gcol = {(h, c): cums[c][0][:, ALPHA_OFF + h:ALPHA_OFF + h + 1] for h, c in pairs}
    grow = {(h, c): cums[c][1][ALPHA_OFF + h:ALPHA_OFF + h + 1, :] for h, c in pairs}
    decay = {p: jnp.exp(jnp.where(incl, gcol[p] - grow[p], -jnp.inf)) for p in pairs}
    eg = {p: jnp.exp(gcol[p]) for p in pairs}
    g_last = {p: gcol[p][CHUNK - 1:CHUNK, :] for p in pairs}

    kk = {(h, c): _bdot(kbs[h][rs[c]], ks[h][rs[c]], NT) for h, c in pairs}
    qk = {(h, c): _bdot(qs[h][rs[c]], ks[h][rs[c]], NT) for h, c in pairs}
    nmat = {p: jnp.where(strict, kk[p] * decay[p], 0.0) for p in pairs}
    attn = {p: qk[p] * decay[p] for p in pairs}
    eye = (_row_iota((CHUNK, CHUNK)) == _col_iota((CHUNK, CHUNK))).astype(F32)
    tinv = {p: eye - nmat[p] for p in pairs}
    nsplit = {p: _split_bf16(nmat[p]) for p in pairs}
    npow = {p: _dot_split(nsplit[p], nsplit[p]) for p in pairs}
    span = 2
    while True:
        nsplit = {p: _split_bf16(npow[p]) for p in pairs}
        tinv = {p: tinv[p] + _dot_split(_split_bf16(tinv[p]), nsplit[p]) for p in pairs}
        span *= 2
        if span >= CHUNK:
            break
        npow = {p: _dot_split(nsplit[p], nsplit[p]) for p in pairs}
    u = {(h, c): _bdot(tinv[(h, c)], vbs[h][rs[c]]) for h, c in pairs}
    w = {(h, c): _bdot(tinv[(h, c)], kbs[h][rs[c]] * eg[(h, c)]) for h, c in pairs}
    q_dec = {(h, c): qs[h][rs[c]] * eg[(h, c)] for h, c in pairs}
    k_dec_t = {(h, c): (ks[h][rs[c]] * jnp.exp(g_last[(h, c)] - gcol[(h, c)])).T for h, c in pairs}

    s = [s_ref[h] for h in heads]
    outs = [[None] * nc for _ in heads]
    for c in range(nc):
        ws = [_bdot(w[(h, c)], s[h]) for h in heads]
        qd = [_bdot(q_dec[(h, c)], s[h]) for h in heads]
        v_new = [u[(h, c)] - ws[h] for h in heads]
        av = [_bdot(attn[(h, c)], v_new[h]) for h in heads]
        kv = [_bdot(k_dec_t[(h, c)], v_new[h]) for h in heads]
        for h in heads:
            outs[h][c] = qd[h] + av[h]
            s[h] = s[h] * jnp.exp(g_last[(h, c)]) + kv[h]
    for h in heads:
        s_ref[h] = s[h]
        o = jnp.concatenate(outs[h], axis=0)
        o = _rms(o, ng_ref[...]) * _silu(z_ref[:, h * hd:(h + 1) * hd])
        o = jnp.where(valid, o, 0.0)
        o_ref[:, h * hd:(h + 1) * hd] = o.astype(o_ref.dtype)


def _gdn(qkv, z, sm, l, cw, alog, dtb, ng):
    b, tp, _ = qkv.shape
    tt = GDN_NC * CHUNK
    return pl.pallas_call(
        functools.partial(_gdn_body, nc=GDN_NC),
        grid=(b, tp // tt),
        in_specs=[_seq_spec(tt, 3 * W_GRP), _seq_spec(tt, W_GRP), _seq_spec(tt, SMALL_W)]
        + [_layer_spec(p, l) for p in (cw, alog, dtb, ng)],
        out_specs=_seq_spec(tt, W_GRP),
        out_shape=jax.ShapeDtypeStruct((b, tp, W_GRP), BF16),
        scratch_shapes=[pltpu.VMEM((SUBLANES, 3 * W_GRP), F32),
                        pltpu.VMEM((GDN_HEADS, GDN_HEAD_DIM, GDN_HEAD_DIM), F32)],
        compiler_params=_compiler_params(("parallel", "arbitrary")),
        name="gdn",
    )(qkv, z, sm, cw, alog, dtb, ng)


SSD_PAIRS = SSD_HEADS // 2


def _ssd_block(xbc, z, sm, states, row0, alog_ref, dtb_ref, dsk_ref, ng_ref, tick):
    tt = xbc.shape[0]
    nc = tt // CHUNK
    hp = SSD_HEAD_DIM
    gw = W_GRP // SSD_GROUPS
    ppg = SSD_PAIRS // SSD_GROUPS
    expand = (_row_iota((SMALL_W, W_GRP)) - DT_OFF
              == _col_iota((SMALL_W, W_GRP)) >> (hp.bit_length() - 1)).astype(F32)
    lane = _col_iota((CHUNK, LANES))
    lo_half = lane < hp
    incl2 = _row_iota((CHUNK, LANES)) >= (lane & (hp - 1))
    lo_rows = _row_iota((LANES, 1)) < hp

    xs = xbc[:, :W_GRP]
    dt_all = _softplus(sm + dtb_ref[...])
    dt_all = jnp.where(_valid_rows(row0, tt, SMALL_W), dt_all, 0.0)
    adt_all = dt_all * -jnp.exp(alog_ref[...])
    x_in = xs * _dot(dt_all, expand, prec=HI)
    cums = [_chunk_cumsum(adt_all[c * CHUNK:(c + 1) * CHUNK]) for c in range(nc)]
    rs = [slice(c * CHUNK, (c + 1) * CHUNK) for c in range(nc)]
    pairs = [(j, c) for j in range(SSD_PAIRS) for c in range(nc)]
    grp = lambda j: j // ppg

    def head_cols(j, c):
        cum, cum_t = cums[c]
        h0 = DT_OFF + 2 * j
        acol = jnp.where(lo_half, cum[:, h0:h0 + 1], cum[:, h0 + 1:h0 + 2])
        arow = jnp.concatenate([cum_t[h0:h0 + 1, :], cum_t[h0 + 1:h0 + 2, :]], axis=1)
        return acol, arow

    b_all = [xbc[:, W_GRP + g * SSD_STATE:W_GRP + (g + 1) * SSD_STATE] for g in range(SSD_GROUPS)]
    c_all = [xbc[:, W_GRP + (SSD_GROUPS + g) * SSD_STATE:W_GRP + (SSD_GROUPS + g + 1) * SSD_STATE]
             for g in range(SSD_GROUPS)]
    cols = {p: head_cols(*p) for p in pairs}
    a_last = {p: cols[p][0][CHUNK - 1:CHUNK, :] for p in pairs}
    xp = {(j, c): x_in[rs[c], j * LANES:(j + 1) * LANES] for j, c in pairs}

    cb = {(g, c): _bdot(c_all[g][rs[c]], b_all[g][rs[c]], NT) for g in range(SSD_GROUPS) for c in range(nc)}
    cb2 = {k: jnp.concatenate([v, v], axis=1) for k, v in cb.items()}
    lm = {p: cb2[(grp(p[0]), p[1])] * jnp.exp(jnp.where(incl2, cols[p][0] - cols[p][1], -jnp.inf))
          for p in pairs}
    x_bd = {p: jnp.concatenate([jnp.where(lo_half, xp[p], 0.0), jnp.where(lo_half, 0.0, xp[p])], axis=0)
            for p in pairs}
    y_diag = {p: _bdot(lm[p], x_bd[p]) for p in pairs}
    tick()
    st = {(j, c): _bdot((xp[(j, c)] * jnp.exp(a_last[(j, c)] - cols[(j, c)][0])).T, b_all[grp(j)][rs[c]])
          for j, c in pairs}
    tick()
    s_in = {}
    new_states = []
    for j in range(SSD_PAIRS):
        s = states[j]
        for c in range(nc):
            s_in[(j, c)] = s
            e_last = jnp.exp(a_last[(j, c)])
            s = s * jnp.where(lo_rows, e_last[:, 0:1], e_last[:, hp:hp + 1]) + st[(j, c)]
        new_states.append(s)
    y = {(j, c): y_diag[(j, c)] + _bdot(c_all[grp(j)][rs[c]], s_in[(j, c)], NT) * jnp.exp(cols[(j, c)][0])
         for j, c in pairs}
    tick()
    y = jnp.concatenate([jnp.concatenate([y[(j, c)] for j in range(SSD_PAIRS)], axis=1)
                         for c in range(nc)], axis=0)
    y = (y + dsk_ref[...] * xs) * _silu(z)
    y = jnp.concatenate(
        [_rms(y[:, g * gw:(g + 1) * gw], ng_ref[:, g * gw:(g + 1) * gw]) for g in range(SSD_GROUPS)],
        axis=1)
    return jnp.where(_valid_rows(row0, tt, W_GRP), y, 0.0), new_states


OUT_TM = 704
OUT_PIECE = 256


def _ssd_outproj_body(z_ref, xbc_ref, sm_ref, cw_ref, cb_ref, alog_ref, dtb_ref, dsk_ref, ng_ref,
                      h_ref, ya_ref, yb_ref, yd_ref, ud_ref, s5d_ref, wglu_ref, s5ng_ref, wo_ref,
                      o_ref, yc_ref, prev_ref, s_ref, *, tiles_per_seq):
    i = pl.program_id(0)
    last = pl.num_programs(0) - 2

    @pl.when(i == 0)
    def _():
        yc_ref[...] = jnp.zeros_like(yc_ref)
        prev_ref[...] = jnp.zeros_like(prev_ref)
        s_ref[...] = jnp.zeros_like(s_ref)

    yc_prev = yc_ref.at[(i + 1) % 2]
    yc_cur = yc_ref.at[i % 2]
    tok = lambda ref: jnp.concatenate([ref[q].reshape(OUT_TM, LANES) for q in range(S5_OCT)], axis=1)
    acc = {}

    def s5_epilogue():
        yd = tok(yd_ref) + s5d_ref[...] * tok(ud_ref)
        yd = _gelu_tanh(yd)
        yd = yd * jax.nn.sigmoid(_dot(yd.astype(BF16), wglu_ref[...]))
        acc["yd"] = _rms(yd, s5ng_ref[...]).astype(BF16)

    def proj_piece(col, grp):
        cols = slice(col * OUT_PIECE, (col + 1) * OUT_PIECE)
        rows = slice(grp * W_GRP, (grp + 1) * W_GRP)
        lhs = (ya_ref, yb_ref, yc_prev, None)[grp]
        lhs = acc["yd"] if lhs is None else lhs[...]
        part = _dot(lhs, wo_ref[rows, cols])
        if grp == 0:
            acc[col] = h_ref[:, cols] + part
        elif grp < 3:
            acc[col] = acc[col] + part
        else:
            o_ref[:, cols] = acc[col] + part

    pieces = [s5_epilogue] + [functools.partial(proj_piece, col, grp)
                              for col in range(D_MODEL // OUT_PIECE) for grp in range(4)]
    todo = iter(pieces)

    def tick():
        nxt = next(todo, None)
        if nxt is not None:
            nxt()

    tile = jnp.minimum(i, last) % tiles_per_seq
    first = tile == 0
    prev8 = jnp.where(first, 0.0, prev_ref[...])
    states = [jnp.where(first, 0.0, s_ref[j]) for j in range(SSD_PAIRS)]
    nchunks = OUT_TM // CHUNK
    blocks = [slice(c0 * CHUNK, min(c0 + SSD_NC, nchunks) * CHUNK) for c0 in range(0, nchunks, SSD_NC)]
    for rs in blocks:
        x = xbc_ref[rs, :]
        act = _silu(_causal_conv4(x, prev8, cw_ref) + cb_ref[...])
        prev8 = x[x.shape[0] - SUBLANES:]
        tick()
        y, states = _ssd_block(act, z_ref[rs, :], sm_ref[rs, :], states, tile * OUT_TM + rs.start,
                               alog_ref, dtb_ref, dsk_ref, ng_ref, tick)
        yc_cur[rs, :] = y.astype(yc_cur.dtype)
        tick()
    for _ in pieces:
        tick()
    prev_ref[...] = prev8
    for j in range(SSD_PAIRS):
        s_ref[j] = states[j]


def _ssd_outproj(z, xbc, sm, h, ya, yb, yd, ud, l, ssd_p, out_p):
    b, tp, _ = h.shape
    tps = tp // OUT_TM
    nt = b * tps

    def cur(i):
        t = jnp.minimum(i, nt - 1)
        return t // tps, t % tps

    def lag(i):
        t = jnp.maximum(i - 1, 0)
        return t // tps, t % tps

    seq = lambda ncols, at: pl.BlockSpec((None, OUT_TM, ncols), lambda i: at(i) + (0,))
    oct_spec = pl.BlockSpec((S5_OCT, OUT_TM // S5_SUB, None, S5_SUB, LANES),
                            lambda i: (0, lag(i)[1], lag(i)[0], 0, 0))
    return pl.pallas_call(
        functools.partial(_ssd_outproj_body, tiles_per_seq=tps),
        grid=(nt + 1,),
        in_specs=[seq(W_GRP, cur), seq(XBC_W, cur), seq(SMALL_W, cur)] + [_layer_spec(p, l) for p in ssd_p]
        + [seq(D_MODEL, lag), seq(W_GRP, lag), seq(W_GRP, lag), oct_spec, oct_spec]
        + [_layer_spec(p, l) for p in out_p],
        out_specs=seq(D_MODEL, lag),
        out_shape=jax.ShapeDtypeStruct(h.shape, F32),
        scratch_shapes=[pltpu.VMEM((2, OUT_TM, W_GRP), BF16), pltpu.VMEM((SUBLANES, XBC_W), F32),
                        pltpu.VMEM((SSD_PAIRS, 2 * SSD_HEAD_DIM, SSD_STATE), F32)],
        compiler_params=_compiler_params(("arbitrary",)),
        name="ssd_outproj",
    )(z, xbc, sm, *ssd_p, h, ya, yb, yd, ud, *out_p)


def _s5_discretise(a_re, a_im, log_dt):
    lam_re = jnp.minimum(a_re, -1e-4)
    lam_im = a_im
    dt = jnp.exp(log_dt)
    mag = jnp.exp(dt * lam_re)
    ab_re = mag * jnp.cos(dt * lam_im)
    ab_im = mag * jnp.sin(dt * lam_im)
    den = lam_re * lam_re + lam_im * lam_im
    f_re = ((ab_re - 1.0) * lam_re + ab_im * lam_im) / den
    f_im = (ab_im * lam_re - (ab_re - 1.0) * lam_im) / den
    return ab_re, ab_im, f_re, f_im


def _s5_prep_body(ar_ref, ai_ref, ld_ref, arc_ref, aic_ref, ldc_ref, btr_ref, bti_ref, ctr_ref, cti_ref,
                  w1_ref, wc_ref, apow_ref):
    ab_re, ab_im, f_re, f_im = _s5_discretise(ar_ref[...], ai_ref[...], ld_ref[...])
    abc_re, abc_im, _, _ = _s5_discretise(arc_ref[...], aic_ref[...], ldc_ref[...])
    bb_re = f_re * btr_ref[...] - f_im * bti_ref[...]
    bb_im = f_re * bti_ref[...] + f_im * btr_ref[...]
    ct_re = ctr_ref[...]
    ct_im = cti_ref[...]

    pows = [(jnp.ones_like(ab_re), jnp.zeros_like(ab_re))]
    for _ in range(S5_SUB):
        p_re, p_im = pows[-1]
        pows.append((p_re * ab_re - p_im * ab_im, p_re * ab_im + p_im * ab_re))
    x_re = jnp.concatenate([bb_re * pows[S5_SUB - 1 - s][0] - bb_im * pows[S5_SUB - 1 - s][1]
                            for s in range(S5_SUB)], axis=0)
    x_im = jnp.concatenate([bb_re * pows[S5_SUB - 1 - s][1] + bb_im * pows[S5_SUB - 1 - s][0]
                            for s in range(S5_SUB)], axis=0)
    taps = _dot(x_re, ct_re, prec=HI) - _dot(x_im, ct_im, prec=HI)
    cols = []
    for l in range(S5_SUB):
        up = (S5_SUB - 1 - l) * LANES
        cols.append(taps[up:] if up == 0 else
                    jnp.concatenate([taps[up:], jnp.zeros((up, LANES), F32)], axis=0))
    w1_ref[...] = jnp.concatenate(cols + [x_re, x_im], axis=1).astype(w1_ref.dtype)
    apow_ref[...] = jnp.concatenate(pows[S5_SUB], axis=1)

    q_re, q_im = abc_re, abc_im
    wc_re, wc_im = [], []
    for l in range(S5_SUB):
        wc_re.append(ct_re * q_re - ct_im * q_im)
        wc_im.append(-(ct_re * q_im + ct_im * q_re))
        q_re, q_im = q_re * abc_re - q_im * abc_im, q_re * abc_im + q_im * abc_re
    wc_ref[...] = jnp.concatenate([jnp.concatenate(wc_re, axis=1), jnp.concatenate(wc_im, axis=1)],
                                  axis=0).astype(wc_ref.dtype)


def _block_diag(w):
    s, i, j = w.shape[-3:]
    eye = jnp.eye(s, dtype=w.dtype)
    return (w[..., :, :, None, :] * eye[:, None, :, None]).reshape(w.shape[:-3] + (s * i, s * j))


def _s5_prep(a_re, a_im, log_dt, b_re, b_im, c_re, c_im):
    depth = a_re.shape[0]
    row = lambda v: v.reshape(depth, S5_OCT, 1, S5_NST)
    col = lambda v: v.reshape(depth, S5_OCT, S5_NST, 1)
    ld = jnp.repeat(log_dt, S5_STATE, axis=1)
    octs = lambda w: _block_diag(jnp.swapaxes(w, 2, 3).reshape(
        (depth, S5_OCT, S5_OCT_GROUPS) + (w.shape[3], w.shape[2])))

    def ospec(shape):
        return pl.BlockSpec((None, None) + shape, lambda d, q: (d, q) + (0,) * len(shape))

    outs = [((S5_VEC, S5_VEC + 2 * S5_NST), BF16), ((2 * S5_NST, S5_VEC), BF16), ((1, 2 * S5_NST), F32)]
    return pl.pallas_call(
        _s5_prep_body,
        grid=(depth, S5_OCT),
        in_specs=[ospec((1, S5_NST))] * 3 + [ospec((S5_NST, 1))] * 3
        + [ospec((LANES, S5_NST))] * 2 + [ospec((S5_NST, LANES))] * 2,
        out_specs=[ospec(s) for s, _ in outs],
        out_shape=[jax.ShapeDtypeStruct((depth, S5_OCT) + s, d) for s, d in outs],
        compiler_params=_compiler_params(("parallel", "parallel")),
        name="s5_prep",
    )(row(a_re), row(a_im), row(ld), col(a_re), col(a_im), col(ld),
      octs(b_re), octs(b_im), octs(c_re), octs(c_im))


S5_RT = 1056
S5_SCAN_UNROLL = 4


def _s5_front(u_ref, w1_ref, loc_ref):
    u = jnp.concatenate([u_ref[pl.ds(s, S5_RT, stride=S5_SUB), :] for s in range(S5_SUB)], axis=1)
    z = _dot(u.astype(BF16), w1_ref[...])
    loc_ref[...] = z[:, S5_VEC:]
    return z[:, :S5_VEC]


def _s5_back(y_intra, wc_ref, apow_ref, y_ref, loc_ref, sin_ref, st_ref):
    nb = st_ref.shape[0]
    a_re = apow_ref[:, :S5_NST]
    a_im = apow_ref[:, S5_NST:]

    def step(c, carry):
        s_re, s_im = carry
        r0 = pl.multiple_of(c * nb, nb)
        sin_ref[pl.ds(r0, nb), :S5_NST] = s_re
        sin_ref[pl.ds(r0, nb), S5_NST:] = s_im
        n_re = a_re * s_re - a_im * s_im + loc_ref[pl.ds(r0, nb), :S5_NST]
        n_im = a_re * s_im + a_im * s_re + loc_ref[pl.ds(r0, nb), S5_NST:]
        return n_re, n_im

    s_re, s_im = lax.fori_loop(0, S5_RT // nb, step, (st_ref[:, :S5_NST], st_ref[:, S5_NST:]),
                               unroll=S5_SCAN_UNROLL)
    st_ref[:, :S5_NST] = s_re
    st_ref[:, S5_NST:] = s_im
    y = y_intra + _dot(sin_ref[...].astype(BF16), wc_ref[...])
    for l in range(S5_SUB):
        y_ref[pl.ds(l, S5_RT, stride=S5_SUB), :] = y[:, l * LANES:(l + 1) * LANES]


def _s5_body(u_ref, w1_ref, wc_ref, apow_ref, y_ref, loc_ref, sin_ref, st_ref):
    @pl.when(pl.program_id(1) == 0)
    def _():
        st_ref[...] = jnp.zeros_like(st_ref)

    y_intra = _s5_front(u_ref, w1_ref, loc_ref)
    _s5_back(y_intra, wc_ref, apow_ref, y_ref, loc_ref, sin_ref, st_ref)


def _s5_scan(u, l, w1, wc, apow, nb):
    noct, rows, _ = u.shape

    def wspec(arr):
        shape = arr.shape[2:]
        return pl.BlockSpec((None, None) + shape, lambda q, r: (l, q) + (0,) * len(shape))

    tile = pl.BlockSpec((None, S5_RT * S5_SUB, LANES), lambda q, r: (q, r, 0))
    return pl.pallas_call(
        _s5_body,
        grid=(noct, rows // (S5_RT * S5_SUB)),
        in_specs=[tile, wspec(w1), wspec(wc), wspec(apow)],
        out_specs=tile,
        out_shape=jax.ShapeDtypeStruct(u.shape, F32),
        scratch_shapes=[pltpu.VMEM((S5_RT, 2 * S5_NST), F32), pltpu.VMEM((S5_RT, 2 * S5_NST), F32),
                        pltpu.VMEM((nb, 2 * S5_NST), F32)],
        compiler_params=_compiler_params(("parallel", "arbitrary")),
        name="s5_scan",
    )(u, w1, wc, apow)


def _final_norm_body(h_ref, g_ref, o_ref):
    skip = h_ref.shape[0] - o_ref.shape[0]
    o_ref[...] = _rms(h_ref[skip:, :], g_ref[...])


def _final_norm(h, g, seq):
    b, tp, d = h.shape
    return pl.pallas_call(
        _final_norm_body,
        grid=(b,),
        in_specs=[pl.BlockSpec((None, tp, d), lambda i: (i, 0, 0)), pl.BlockSpec((1, d), lambda i: (0, 0))],
        out_specs=pl.BlockSpec((None, seq, d), lambda i: (i, 0, 0)),
        out_shape=jax.ShapeDtypeStruct((b, seq, d), F32),
        compiler_params=_compiler_params(("parallel",)),
        name="final_norm",
    )(h, g)


def _small_rows(vals, off):
    depth, n = vals.shape
    return jnp.zeros((depth, 1, SMALL_W), F32).at[:, 0, off:off + n].set(vals)


def _pack_w_in(w_in):
    sizes = (W_GRP, W_GRP, 3 * W_GRP, W_GRP, GDN_HEADS, GDN_HEADS, W_GRP, XBC_W, SSD_HEADS, W_GRP)
    offs = np.concatenate([[0], np.cumsum(sizes)])
    a_x, a_gate, b_qkv, b_z, b_beta, b_alpha, c_z, c_xbc, c_dt, d_u = (
        w_in[..., offs[i]:offs[i + 1]] for i in range(len(sizes)))
    pad = lambda n: jnp.zeros(w_in.shape[:-1] + (n,), w_in.dtype)
    small = jnp.concatenate([b_beta, b_alpha, c_dt, pad(SMALL_W - DT_OFF - SSD_HEADS)], axis=-1)
    return jnp.concatenate([a_x, a_gate, b_qkv, b_z, c_z, c_xbc, small, d_u], axis=-1)


def kernel(x, meta_tokens, ffn1_norm, ffn1_w_gate, ffn1_w_up, ffn1_w_down, mix_norm, w_in, w_out,
           lru_conv_w, lru_conv_b, lru_w_a, lru_b_a, lru_w_i, lru_b_i, lru_lambda, lru_norm,
           gdn_conv_w, gdn_a_log, gdn_dt_bias, gdn_norm,
           ssd_conv_w, ssd_conv_b, ssd_a_log, ssd_dt_bias, ssd_d, ssd_norm,
           s5_a_re, s5_a_im, s5_log_dt, s5_b_re, s5_b_im, s5_c_re, s5_c_im, s5_d, s5_w_glu, s5_norm,
           ffn2_norm, ffn2_w_gate, ffn2_w_up, ffn2_w_down, final_norm):
    bsz, seq, d = x.shape
    depth = w_in.shape[0]
    tp = LEAD_PAD + N_META + seq
    m = bsz * tp
    nsub = tp // S5_SUB
    assert d == D_MODEL and bsz == SUBLANES and m % FFN_TM == 0
    assert all(tp % t == 0 for t in (PROJ_TM, GDN_NC * CHUNK, OUT_TM))
    assert (nsub * bsz) % S5_RT == 0
    assert (BETA_OFF, ALPHA_OFF, DT_OFF) == (0, GDN_HEADS, 2 * GDN_HEADS)

    meta = jnp.broadcast_to(meta_tokens.astype(x.dtype)[None], (bsz, N_META, d))
    h = jnp.concatenate([jnp.zeros((bsz, LEAD_PAD, d), x.dtype), meta, x], axis=1)

    rows = lambda v: v.reshape(depth, 1, -1).astype(F32)
    bf = lambda w: w.astype(BF16)
    ffn1 = (rows(ffn1_norm), bf(ffn1_w_gate), bf(ffn1_w_up), bf(ffn1_w_down))
    ffn2 = (rows(ffn2_norm), bf(ffn2_w_gate), bf(ffn2_w_up), bf(ffn2_w_down))
    proj = (rows(mix_norm), _pack_w_in(bf(w_in)))
    lru_p = (lru_conv_w, rows(lru_conv_b),
             jnp.concatenate([_block_diag(bf(lru_w_a)), _block_diag(bf(lru_w_i))], axis=-1),
             rows(jnp.concatenate([lru_b_a, lru_b_i], axis=-1)), rows(lru_lambda), rows(lru_norm))
    gdn_p = (gdn_conv_w, _small_rows(gdn_a_log, ALPHA_OFF), _small_rows(gdn_dt_bias, ALPHA_OFF), rows(gdn_norm))
    ssd_p = (ssd_conv_w, rows(ssd_conv_b), _small_rows(ssd_a_log, DT_OFF), _small_rows(ssd_dt_bias, DT_OFF),
             rows(jnp.repeat(ssd_d, SSD_HEAD_DIM, axis=1)), rows(ssd_norm))
    s5_w = _s5_prep(s5_a_re, s5_a_im, s5_log_dt, s5_b_re, s5_b_im, s5_c_re, s5_c_im)
    out_p = (rows(s5_d), bf(s5_w_glu), rows(s5_norm), bf(w_out))

    flat = lambda t: t.reshape(m, d)
    seq3 = lambda t: t.reshape(bsz, tp, d)
    for l in range(depth):
        h = seq3(_ffn(flat(h), l, *ffn1))
        b_qkv, b_z, c_z, c_xbc, small, d_u, y_a = _inproj_lru(h, l, *proj, *lru_p)
        y_b = _gdn(b_qkv, b_z, small, l, *gdn_p)
        y_d = _s5_scan(d_u.reshape(S5_OCT, nsub * bsz * S5_SUB, LANES), l, *s5_w, bsz).reshape(d_u.shape)
        h = _ssd_outproj(c_z, c_xbc, small, h, y_a, y_b, y_d, d_u, l, ssd_p, out_p)
        h = seq3(_ffn(flat(h), l, *ffn2))

    return _final_norm(h, final_norm.reshape(1, d), seq)
```
